```python
import jax, jax.numpy as jnp
from jax import lax
import numpy as np

D_MODEL = 1024
BATCH = 2
SEQ = 8192
DEPTH = 2

PLE_DIM = 256
N_BRANCH = 4
BRANCH_WIDTH = 256
N_GROUPS = 4
GROUP_DIM = BRANCH_WIDTH // N_GROUPS
CHUNK = 128
CONF_K = 31
SHORT_K = 3
FFN_K = 3
D_FF = 2816
POOL_WINDOWS = (2, 4, 8, 16)
EPS = 1e-6

A_COLS = 2 * BRANCH_WIDTH
B_COLS = 2 * BRANCH_WIDTH
C_COLS = 3 * BRANCH_WIDTH
D_COLS = BRANCH_WIDTH
GATE_COLS = N_BRANCH * D_MODEL
IN_COLS = A_COLS + B_COLS + C_COLS + D_COLS + GATE_COLS
SPLITS = (A_COLS, A_COLS + B_COLS, A_COLS + B_COLS + C_COLS, A_COLS + B_COLS + C_COLS + D_COLS)

kernel_name = "hybrid_gated_conv_pool_gmlp_trunk"


def rms_norm(x, g):
    xf = x.astype(jnp.float32)
    y = xf * lax.rsqrt(jnp.mean(xf * xf, axis=-1, keepdims=True) + EPS)
    return (y * g.astype(jnp.float32)).astype(x.dtype)


def layer_norm(x, g, b):
    xf = x.astype(jnp.float32)
    mu = jnp.mean(xf, axis=-1, keepdims=True)
    xc = xf - mu
    var = jnp.mean(xc * xc, axis=-1, keepdims=True)
    y = xc * lax.rsqrt(var + EPS)
    return (y * g.astype(jnp.float32) + b.astype(jnp.float32)).astype(x.dtype)


def causal_dwconv(x, w):
    K, C = w.shape
    return lax.conv_general_dilated(
        x, w[:, None, :].astype(x.dtype), window_strides=(1,), padding=[(K - 1, 0)],
        dimension_numbers=('NWC', 'WIO', 'NWC'), feature_group_count=C)


def gmlp_spatial_gating(z, ln_g, ln_b, w_s, b_s):
    Bn, S, _ = z.shape
    z = jax.nn.gelu(z)
    u, v = jnp.split(z, 2, axis=-1)
    v = layer_norm(v, ln_g, ln_b)
    v = v.reshape(Bn, S // CHUNK, CHUNK, N_GROUPS, GROUP_DIM)
    causal = jnp.tril(jnp.ones((CHUNK, CHUNK), dtype=bool))
    w = jnp.where(causal[None], w_s, 0)
    mixed = jnp.einsum('gts,bnsgc->bntgc', w, v) + b_s.T[None, None, :, :, None]
    return u * mixed.reshape(Bn, S, BRANCH_WIDTH)


def conformer_conv(z, w_dw, b_dw, ln_g, ln_b):
    Bn, S, _ = z.shape
    a, g = jnp.split(z, 2, axis=-1)
    y = causal_dwconv(a * jax.nn.sigmoid(g), w_dw) + b_dw
    y = layer_norm(y.reshape(Bn, S, N_GROUPS, GROUP_DIM), ln_g, ln_b)
    return jax.nn.silu(y).reshape(Bn, S, BRANCH_WIDTH)


def short_gated_conv(z, w):
    b_gate, c_gate, xin = jnp.split(z, 3, axis=-1)
    return b_gate * causal_dwconv(c_gate * xin, w)


def multiscale_pool(z, w_grp, scale):
    Bn, S, _ = z.shape
    zf = z.reshape(Bn, S, N_GROUPS, GROUP_DIM).astype(jnp.float32)
    cs = jnp.cumsum(zf, axis=1)
    t = jnp.arange(S)
    outs = []
    for gi, win in enumerate(POOL_WINDOWS):
        c = cs[:, :, gi]
        prev = jnp.pad(c, ((0, 0), (win, 0), (0, 0)))[:, :S]
        cnt = jnp.minimum(t + 1, win).astype(jnp.float32)[None, :, None]
        outs.append((c - prev) / cnt - zf[:, :, gi])
    pooled = jnp.stack(outs, axis=2).astype(z.dtype)
    y = jnp.einsum('bsgc,gcd->bsgd', pooled, w_grp) * scale
    return y.reshape(Bn, S, BRANCH_WIDTH)


def conv_glu_ffn(h, w_up, w_conv, b_conv, w_down):
    up = causal_dwconv(h @ w_up, w_conv) + b_conv
    gate, val = jnp.split(up, 2, axis=-1)
    return (jax.nn.silu(gate) * val) @ w_down


def setup_inputs(seed: int = 0) -> dict:
    key = jax.random.key(seed)
    ks = jax.random.split(key, 32)
    nrm = lambda k, shape, s: jax.random.normal(k, shape, jnp.float32) * s
    L, D, BW, G, GD = DEPTH, D_MODEL, BRANCH_WIDTH, N_GROUPS, GROUP_DIM
    return {
        "x": nrm(ks[0], (BATCH, SEQ, D), 1.0),
        "p": nrm(ks[1], (L, BATCH, SEQ, PLE_DIM), 1.0),
        "g_mix": 1.0 + nrm(ks[2], (L, D), 0.02),
        "w_in": nrm(ks[3], (L, D, IN_COLS), D ** -0.5),
        "gmlp_ln_g": 1.0 + nrm(ks[4], (L, BW), 0.02),
        "gmlp_ln_b": nrm(ks[5], (L, BW), 0.02),
        "gmlp_w_s": nrm(ks[6], (L, G, CHUNK, CHUNK), CHUNK ** -0.5),
        "gmlp_b_s": 1.0 + nrm(ks[7], (L, G, CHUNK), 0.02),
        "conf_w_dw": nrm(ks[8], (L, CONF_K, BW), CONF_K ** -0.5),
        "conf_b_dw": nrm(ks[9], (L, BW), 0.02),
        "conf_ln_g": 1.0 + nrm(ks[10], (L, G, GD), 0.02),
        "conf_ln_b": nrm(ks[11], (L, G, GD), 0.02),
        "short_w": nrm(ks[12], (L, SHORT_K, BW), SHORT_K ** -0.5),
        "pool_w": nrm(ks[13], (L, G, GD, GD), GD ** -0.5),
        "pool_scale": 1.0 + nrm(ks[14], (L, G, GD), 0.02),
        "w_branch": nrm(ks[15], (L, N_BRANCH, BW, D), BW ** -0.5),
        "w_out": nrm(ks[16], (L, D, D), D ** -0.5),
        "g_ffn": 1.0 + nrm(ks[17], (L, D), 0.02),
        "ffn_w_up": nrm(ks[18], (L, D, 2 * D_FF), D ** -0.5),
        "ffn_w_conv": nrm(ks[19], (L, FFN_K, 2 * D_FF), FFN_K ** -0.5),
        "ffn_b_conv": nrm(ks[20], (L, 2 * D_FF), 0.02),
        "ffn_w_down": nrm(ks[21], (L, D_FF, D), D_FF ** -0.5),
        "g_ple": 1.0 + nrm(ks[22], (L, D), 0.02),
        "ple_w_gate": nrm(ks[23], (L, D, D), D ** -0.5),
        "ple_w_proj": nrm(ks[24], (L, PLE_DIM, D), PLE_DIM ** -0.5),
        "g_final": 1.0 + nrm(ks[25], (D,), 0.02),
    }


def reference(x, p, g_mix, w_in, gmlp_ln_g, gmlp_ln_b, gmlp_w_s, gmlp_b_s,
              conf_w_dw, conf_b_dw, conf_ln_g, conf_ln_b, short_w, pool_w, pool_scale,
              w_branch, w_out, g_ffn, ffn_w_up, ffn_w_conv, ffn_b_conv, ffn_w_down,
              g_ple, ple_w_gate, ple_w_proj, g_final):
    Bn, S, _ = x.shape
    for i in range(DEPTH):
        h = rms_norm(x, g_mix[i])
        proj = h @ w_in[i]
        za, zb, zc, zd, zg = jnp.split(proj, SPLITS, axis=-1)
        ya = gmlp_spatial_gating(za, gmlp_ln_g[i], gmlp_ln_b[i], gmlp_w_s[i], gmlp_b_s[i])
        yb = conformer_conv(zb, conf_w_dw[i], conf_b_dw[i], conf_ln_g[i], conf_ln_b[i])
        yc = short_gated_conv(zc, short_w[i])
        yd = multiscale_pool(zd, pool_w[i], pool_scale[i])
        branches = jnp.stack([ya, yb, yc, yd], axis=2)
        br = jnp.einsum('bskc,kcd->bskd', branches, w_branch[i])
        gates = jax.nn.sigmoid(zg.reshape(Bn, S, N_BRANCH, D_MODEL))
        merged = jnp.einsum('bskd,bskd->bsd', br, gates)
        x = x + merged @ w_out[i]
        h = rms_norm(x, g_ffn[i])
        x = x + conv_glu_ffn(h, ffn_w_up[i], ffn_w_conv[i], ffn_b_conv[i], ffn_w_down[i])
        h = rms_norm(x, g_ple[i])
        x = x + jax.nn.sigmoid(h @ ple_w_gate[i]) * (p[i] @ ple_w_proj[i])
    return rms_norm(x, g_final)
```

```python
import functools

import jax
import jax.numpy as jnp
from jax import lax
from jax.experimental import pallas as pl
from jax.experimental.pallas import tpu as pltpu

D_MODEL = 1024
PLE_DIM = 256
N_BRANCH = 4
BW = 256
N_GROUPS = 4
GROUP_DIM = BW // N_GROUPS
CHUNK = 128
CONF_K = 31
SHORT_K = 3
FFN_K = 3
D_FF = 2816
POOL_WINDOWS = (2, 4, 8, 16)
EPS = 1e-6

A_OFF, B_OFF, C_OFF, D_OFF, G_OFF = 0, 2 * BW, 4 * BW, 7 * BW, 8 * BW
IN_COLS = G_OFF + N_BRANCH * D_MODEL

TM = 512
CONF_HALO = 32
SHORT_HALO = 8
POOL_HALO = 16
FFN_COLS = 256
VMEM_LIMIT_BYTES = 56 * 1024 * 1024

BF16 = jnp.bfloat16
F32 = jnp.float32


def _dot(a, b):
    return jnp.dot(a, b, preferred_element_type=F32)


def _rms_norm(x, g):
    ms = jnp.mean(x * x, axis=-1, keepdims=True)
    return x * lax.rsqrt(ms + EPS) * g


def _mixer_kernel(x_ref, g_ref, win_ref, lng_ref, lnb_ref, ws_ref, bs_ref,
                  cw_ref, cb_ref, clg_ref, clb_ref, sw_ref, pw_ref, ps_ref,
                  wbr_ref, wout_ref, o_ref,
                  h_buf, y_buf, conf_buf, short_buf, pool_buf):
    s = pl.program_id(1)

    @pl.when(s == 0)
    def _():
        conf_buf[0:CONF_HALO, :] = jnp.zeros((CONF_HALO, BW), F32)
        short_buf[0:SHORT_HALO, :] = jnp.zeros((SHORT_HALO, BW), F32)
        pool_buf[0:POOL_HALO, :] = jnp.zeros((POOL_HALO, BW), F32)

    h_buf[...] = _rms_norm(x_ref[...], g_ref[...]).astype(BF16)

    za = jax.nn.gelu(_dot(h_buf[...], win_ref[:, A_OFF:A_OFF + 2 * BW]))
    u = za[:, :BW]
    v = za[:, BW:]
    mu = jnp.mean(v, axis=-1, keepdims=True)
    vc = v - mu
    var = jnp.mean(vc * vc, axis=-1, keepdims=True)
    v = (vc * lax.rsqrt(var + EPS) * lng_ref[...] + lnb_ref[...]).astype(BF16)
    row = lax.broadcasted_iota(jnp.int32, (CHUNK, N_GROUPS * CHUNK), 0)
    col = lax.broadcasted_iota(jnp.int32, (CHUNK, N_GROUPS * CHUNK), 1)
    w_cat = jnp.where((col & (CHUNK - 1)) <= row, ws_ref[...], jnp.zeros((), BF16))
    lane_group = lax.broadcasted_iota(jnp.int32, (CHUNK, BW), 1) >> 6
    for n in range(TM // CHUNK):
        vn = v[n * CHUNK:(n + 1) * CHUNK, :]
        v_bd = jnp.concatenate(
            [jnp.where(lane_group == g, vn, jnp.zeros((), BF16)) for g in range(N_GROUPS)],
            axis=0)
        mixed = _dot(w_cat, v_bd) + bs_ref[...]
        y_buf[0, n * CHUNK:(n + 1) * CHUNK, :] = (
            u[n * CHUNK:(n + 1) * CHUNK, :] * mixed).astype(BF16)

    zb = _dot(h_buf[...], win_ref[:, B_OFF:B_OFF + 2 * BW])
    conf_buf[CONF_HALO:CONF_HALO + TM, :] = zb[:, :BW] * jax.nn.sigmoid(zb[:, BW:])
    base = CONF_HALO - (CONF_K - 1)
    acc = conf_buf[base:base + TM, :] * cw_ref[0:1, :]
    for k in range(1, CONF_K):
        acc = acc + conf_buf[base + k:base + k + TM, :] * cw_ref[k:k + 1, :]
    yb = acc + cb_ref[...]
    conf_buf[0:CONF_HALO, :] = conf_buf[TM:TM + CONF_HALO, :]
    gi = lax.broadcasted_iota(jnp.int32, (BW, BW), 0) >> 6
    gj = lax.broadcasted_iota(jnp.int32, (BW, BW), 1) >> 6
    g_avg = jnp.where(gi == gj, 1.0 / GROUP_DIM, 0.0).astype(BF16)
    ybc = yb - _dot(yb.astype(BF16), g_avg)
    gvar = _dot((ybc * ybc).astype(BF16), g_avg)
    ybn = ybc * lax.rsqrt(gvar + EPS) * clg_ref[...] + clb_ref[...]
    y_buf[1] = (ybn * jax.nn.sigmoid(ybn)).astype(BF16)

    zc = _dot(h_buf[...], win_ref[:, C_OFF:C_OFF + 3 * BW])
    short_buf[SHORT_HALO:SHORT_HALO + TM, :] = zc[:, BW:2 * BW] * zc[:, 2 * BW:]
    base = SHORT_HALO - (SHORT_K - 1)
    conv = short_buf[base:base + TM, :] * sw_ref[0:1, :]
    for k in range(1, SHORT_K):
        conv = conv + short_buf[base + k:base + k + TM, :] * sw_ref[k:k + 1, :]
    short_buf[0:SHORT_HALO, :] = short_buf[TM:TM + SHORT_HALO, :]
    y_buf[2] = (zc[:, :BW] * conv).astype(BF16)

    zd = _dot(h_buf[...], win_ref[:, D_OFF:D_OFF + BW])
    pool_buf[POOL_HALO:POOL_HALO + TM, :] = zd
    t_pos = s * TM + lax.broadcasted_iota(jnp.int32, (TM, 128), 0) + 1
    lane = lax.broadcasted_iota(jnp.int32, (TM, 128), 1)
    pooled = []
    for half in range(2):
        w_lo, w_hi = POOL_WINDOWS[2 * half], POOL_WINDOWS[2 * half + 1]
        cols = slice(128 * half, 128 * (half + 1))
        run = zd[:, cols]
        for j in range(1, w_hi):
            run = run + pool_buf[POOL_HALO - j:POOL_HALO - j + TM, cols]
            if j == w_lo - 1:
                lo_sum = run
        low = lane < GROUP_DIM
        win_sum = jnp.where(low, lo_sum, run)
        cnt = jnp.minimum(t_pos, jnp.where(low, w_lo, w_hi)).astype(F32)
        pooled.append(win_sum / cnt - zd[:, cols])
    pool_buf[0:POOL_HALO, :] = pool_buf[TM:TM + POOL_HALO, :]
    pooled = jnp.concatenate(pooled, axis=1).astype(BF16)
    y_buf[3] = (_dot(pooled, pw_ref[...]) * ps_ref[...]).astype(BF16)

    merged = None
    for k in range(N_BRANCH):
        br = _dot(y_buf[k], wbr_ref[k])
        zg = _dot(h_buf[...], win_ref[:, G_OFF + k * D_MODEL:G_OFF + (k + 1) * D_MODEL])
        term = br * jax.nn.sigmoid(zg)
        merged = term if merged is None else merged + term
    o_ref[...] = x_ref[...] + _dot(merged.astype(BF16), wout_ref[...])


def _ffn_kernel(x_ref, p_ref, gffn_ref, wup_ref, wconv_ref, bconv_ref, wdown_ref,
                gple_ref, wpg_ref, wpp_ref, gfin_ref, o_ref,
                h_buf, act_buf, up_buf, carry_buf, *, final_norm):
    s = pl.program_id(1)

    @pl.when(s == 0)
    def _():
        carry_buf[...] = jnp.zeros(carry_buf.shape, F32)

    x = x_ref[...]
    h_buf[...] = _rms_norm(x, gffn_ref[...]).astype(BF16)

    base = SHORT_HALO - (FFN_K - 1)
    for j in range(D_FF // FFN_COLS):
        parts = []
        for part in range(2):
            c0 = part * D_FF + j * FFN_COLS
            cols = slice(c0, c0 + FFN_COLS)
            up_buf[part, 0:SHORT_HALO, :] = carry_buf[:, cols]
            up_buf[part, SHORT_HALO:SHORT_HALO + TM, :] = _dot(h_buf[...], wup_ref[:, cols])
            conv = up_buf[part, base:base + TM, :] * wconv_ref[0:1, cols]
            for k in range(1, FFN_K):
                conv = conv + up_buf[part, base + k:base + k + TM, :] * wconv_ref[k:k + 1, cols]
            carry_buf[:, cols] = up_buf[part, TM:TM + SHORT_HALO, :]
            parts.append(conv + bconv_ref[:, cols])
        gate, val = parts
        act_buf[:, j * FFN_COLS:(j + 1) * FFN_COLS] = (
            gate * jax.nn.sigmoid(gate) * val).astype(BF16)

    y = x + _dot(act_buf[...], wdown_ref[...])
    h2 = _rms_norm(y, gple_ref[...]).astype(BF16)
    gate = jax.nn.sigmoid(_dot(h2, wpg_ref[...]))
    y = y + gate * _dot(p_ref[...].astype(BF16), wpp_ref[...])
    if final_norm:
        y = _rms_norm(y, gfin_ref[...])
    o_ref[...] = y


def _const_spec(shape, layer):
    nd = len(shape)
    return pl.BlockSpec((None,) + tuple(shape), lambda b, s: (layer,) + (0,) * nd,
                        pipeline_mode=pl.Buffered(1))


def _tile_spec(width):
    return pl.BlockSpec((None, TM, width), lambda b, s: (b, s, 0))


def _compiler_params():
    return pltpu.CompilerParams(dimension_semantics=("arbitrary", "arbitrary"),
                                vmem_limit_bytes=VMEM_LIMIT_BYTES)


def _mixer_call(x, layer, params):
    batch, seq, _ = x.shape
    in_specs = [_tile_spec(D_MODEL)] + [_const_spec(p.shape[1:], layer) for p in params]
    return pl.pallas_call(
        _mixer_kernel,
        grid=(batch, seq // TM),
        in_specs=in_specs,
        out_specs=_tile_spec(D_MODEL),
        out_shape=jax.ShapeDtypeStruct(x.shape, F32),
        scratch_shapes=[
            pltpu.VMEM((TM, D_MODEL), BF16),
            pltpu.VMEM((N_BRANCH, TM, BW), BF16),
            pltpu.VMEM((CONF_HALO + TM, BW), F32),
            pltpu.VMEM((SHORT_HALO + TM, BW), F32),
            pltpu.VMEM((POOL_HALO + TM, BW), F32),
        ],
        compiler_params=_compiler_params(),
        name=f"mixer_l{layer}",
    )(x, *params)


def _ffn_call(x, p, layer, params, gfin, final_norm):
    batch, seq, _ = x.shape
    in_specs = ([_tile_spec(D_MODEL),
                 pl.BlockSpec((None, None, TM, PLE_DIM), lambda b, s: (layer, b, s, 0))]
                + [_const_spec(q.shape[1:], layer) for q in params]
                + [pl.BlockSpec(gfin.shape, lambda b, s: (0, 0), pipeline_mode=pl.Buffered(1))])
    return pl.pallas_call(
        functools.partial(_ffn_kernel, final_norm=final_norm),
        grid=(batch, seq // TM),
        in_specs=in_specs,
        out_specs=_tile_spec(D_MODEL),
        out_shape=jax.ShapeDtypeStruct(x.shape, F32),
        scratch_shapes=[
            pltpu.VMEM((TM, D_MODEL), BF16),
            pltpu.VMEM((TM, D_FF), BF16),
            pltpu.VMEM((2, SHORT_HALO + TM, FFN_COLS), F32),
            pltpu.VMEM((SHORT_HALO, 2 * D_FF), F32),
        ],
        compiler_params=_compiler_params(),
        name=f"ffn_l{layer}",
    )(x, p, *params, gfin)


def kernel(x, p, g_mix, w_in, gmlp_ln_g, gmlp_ln_b, gmlp_w_s, gmlp_b_s, conf_w_dw, conf_b_dw, conf_ln_g, conf_ln_b, short_w, pool_w, pool_scale, w_branch, w_out, g_ffn, ffn_w_up, ffn_w_conv, ffn_b_conv, ffn_w_down, g_ple, ple_w_gate, ple_w_proj, g_final):
    depth = w_in.shape[0]
    assert x.shape[1] % TM == 0 and TM % CHUNK == 0
    row = lambda a: a.reshape(depth, 1, -1)

    pool_bd = jnp.zeros((depth, BW, BW), F32)
    for g in range(N_GROUPS):
        sl = slice(g * GROUP_DIM, (g + 1) * GROUP_DIM)
        pool_bd = pool_bd.at[:, sl, sl].set(pool_w[:, g])
    mixer_params = [
        row(g_mix),
        w_in.astype(BF16),
        row(gmlp_ln_g), row(gmlp_ln_b),
        gmlp_w_s.transpose(0, 2, 1, 3).reshape(depth, CHUNK, N_GROUPS * CHUNK).astype(BF16),
        jnp.repeat(gmlp_b_s.transpose(0, 2, 1), GROUP_DIM, axis=2),
        conf_w_dw, row(conf_b_dw), row(conf_ln_g), row(conf_ln_b),
        short_w,
        pool_bd.astype(BF16), row(pool_scale),
        w_branch.astype(BF16), w_out.astype(BF16),
    ]
    ffn_params = [
        row(g_ffn), ffn_w_up.astype(BF16), ffn_w_conv, row(ffn_b_conv),
        ffn_w_down.astype(BF16), row(g_ple), ple_w_gate.astype(BF16), ple_w_proj.astype(BF16),
    ]
    gfin = g_final.reshape(1, -1)

    for i in range(depth):
        x = _mixer_call(x, i, mixer_params)
        x = _ffn_call(x, p, i, ffn_params, gfin, final_norm=(i == depth - 1))
    return x
```

```python
import functools

import jax
import jax.numpy as jnp
from jax import lax
from jax.experimental import pallas as pl
from jax.experimental.pallas import tpu as pltpu

D_MODEL = 1024
PLE_DIM = 256
N_BRANCH = 4
BW = 256
N_GROUPS = 4
GROUP_DIM = BW // N_GROUPS
GROUP_SHIFT = GROUP_DIM.bit_length() - 1
CHUNK = 128
CONF_K = 31
SHORT_K = 3
FFN_K = 3
D_FF = 2816
POOL_WINDOWS = (2, 4, 8, 16)
EPS = 1e-6

A_OFF, B_OFF, C_OFF, D_OFF, G_OFF = 0, 2 * BW, 4 * BW, 7 * BW, 8 * BW
IN_COLS = G_OFF + N_BRANCH * D_MODEL

TM = 512
CONF_HALO = 32
SHORT_HALO = 8
POOL_HALO = 16
FFN_COLS = 256
DOWN_GROUP = 2
VMEM_LIMIT_BYTES = 56 * 1024 * 1024

BF16 = jnp.bfloat16
F32 = jnp.float32


def _dot(a, b):
    return jnp.dot(a, b, preferred_element_type=F32)


def _rms_norm(x, g):
    ms = jnp.mean(x * x, axis=-1, keepdims=True)
    return x * lax.rsqrt(ms + EPS) * g


def _mixer_kernel(x_ref, g_ref, win_ref, lng_ref, lnb_ref, ws_ref, bs_ref,
                  cw_ref, cb_ref, clg_ref, clb_ref, sw_ref, pw_ref, ps_ref,
                  wbr_ref, wout_ref, o_ref,
                  h_buf, y_buf, gate_buf, conf_buf, short_buf, pool_buf):
    s = pl.program_id(1)

    @pl.when(s == 0)
    def _():
        conf_buf[0:CONF_HALO, :] = jnp.zeros((CONF_HALO, BW), F32)
        short_buf[0:SHORT_HALO, :] = jnp.zeros((SHORT_HALO, BW), F32)
        pool_buf[0:POOL_HALO, :] = jnp.zeros((POOL_HALO, BW), F32)

    h_buf[...] = _rms_norm(x_ref[...], g_ref[...]).astype(BF16)

    zb = _dot(h_buf[...], win_ref[:, B_OFF:B_OFF + 2 * BW])
    conf_buf[CONF_HALO:CONF_HALO + TM, :] = zb[:, :BW] * jax.nn.sigmoid(zb[:, BW:])
    zc = _dot(h_buf[...], win_ref[:, C_OFF:C_OFF + 3 * BW])
    short_buf[SHORT_HALO:SHORT_HALO + TM, :] = zc[:, BW:2 * BW] * zc[:, 2 * BW:]
    zd = _dot(h_buf[...], win_ref[:, D_OFF:D_OFF + BW])
    pool_buf[POOL_HALO:POOL_HALO + TM, :] = zd
    za = jax.nn.gelu(_dot(h_buf[...], win_ref[:, A_OFF:A_OFF + 2 * BW]))
    for k in range(N_BRANCH):
        zg = _dot(h_buf[...], win_ref[:, G_OFF + k * D_MODEL:G_OFF + (k + 1) * D_MODEL])
        gate_buf[k] = jax.nn.sigmoid(zg)

    base = CONF_HALO - (CONF_K - 1)
    acc = conf_buf[base:base + TM, :] * cw_ref[0:1, :]
    for k in range(1, CONF_K):
        acc = acc + conf_buf[base + k:base + k + TM, :] * cw_ref[k:k + 1, :]
    yb = acc + cb_ref[...]
    conf_buf[0:CONF_HALO, :] = conf_buf[TM:TM + CONF_HALO, :]

    base = SHORT_HALO - (SHORT_K - 1)
    conv = short_buf[base:base + TM, :] * sw_ref[0:1, :]
    for k in range(1, SHORT_K):
        conv = conv + short_buf[base + k:base + k + TM, :] * sw_ref[k:k + 1, :]
    short_buf[0:SHORT_HALO, :] = short_buf[TM:TM + SHORT_HALO, :]
    y_buf[2] = (zc[:, :BW] * conv).astype(BF16)

    t_pos = s * TM + lax.broadcasted_iota(jnp.int32, (TM, 128), 0) + 1
    lane = lax.broadcasted_iota(jnp.int32, (TM, 128), 1)
    pooled = []
    for half in range(2):
        w_lo, w_hi = POOL_WINDOWS[2 * half], POOL_WINDOWS[2 * half + 1]
        cols = slice(128 * half, 128 * (half + 1))
        run = zd[:, cols]
        for j in range(1, w_hi):
            run = run + pool_buf[POOL_HALO - j:POOL_HALO - j + TM, cols]
            if j == w_lo - 1:
                lo_sum = run
        low = lane < GROUP_DIM
        win_sum = jnp.where(low, lo_sum, run)
        cnt = jnp.minimum(t_pos, jnp.where(low, w_lo, w_hi)).astype(F32)
        pooled.append(win_sum / cnt - zd[:, cols])
    pool_buf[0:POOL_HALO, :] = pool_buf[TM:TM + POOL_HALO, :]
    pooled = jnp.concatenate(pooled, axis=1).astype(BF16)

    u = za[:, :BW]
    v = za[:, BW:]
    mu = jnp.mean(v, axis=-1, keepdims=True)
    vc = v - mu
    var = jnp.mean(vc * vc, axis=-1, keepdims=True)
    v = (vc * lax.rsqrt(var + EPS) * lng_ref[...] + lnb_ref[...]).astype(BF16)

    y_buf[3] = (_dot(pooled, pw_ref[...]) * ps_ref[...]).astype(BF16)
    row = lax.broadcasted_iota(jnp.int32, (CHUNK, N_GROUPS * CHUNK), 0)
    col = lax.broadcasted_iota(jnp.int32, (CHUNK, N_GROUPS * CHUNK), 1)
    w_cat = jnp.where((col & (CHUNK - 1)) <= row, ws_ref[...], jnp.zeros((), BF16))
    lane_group = lax.broadcasted_iota(jnp.int32, (CHUNK, BW), 1) >> GROUP_SHIFT
    for n in range(TM // CHUNK):
        vn = v[n * CHUNK:(n + 1) * CHUNK, :]
        v_bd = jnp.concatenate(
            [jnp.where(lane_group == g, vn, jnp.zeros((), BF16)) for g in range(N_GROUPS)],
            axis=0)
        mixed = _dot(w_cat, v_bd) + bs_ref[...]
        y_buf[0, n * CHUNK:(n + 1) * CHUNK, :] = (
            u[n * CHUNK:(n + 1) * CHUNK, :] * mixed).astype(BF16)
    gi = lax.broadcasted_iota(jnp.int32, (BW, BW), 0) >> GROUP_SHIFT
    gj = lax.broadcasted_iota(jnp.int32, (BW, BW), 1) >> GROUP_SHIFT
    g_avg = jnp.where(gi == gj, 1.0 / GROUP_DIM, 0.0).astype(BF16)
    ybc = yb - _dot(yb.astype(BF16), g_avg)
    gvar = _dot((ybc * ybc).astype(BF16), g_avg)
    ybn = ybc * lax.rsqrt(gvar + EPS) * clg_ref[...] + clb_ref[...]
    y_buf[1] = (ybn * jax.nn.sigmoid(ybn)).astype(BF16)

    merged = None
    for k in (2, 3, 0, 1):
        term = _dot(y_buf[k], wbr_ref[k]) * gate_buf[k]
        merged = term if merged is None else merged + term
    o_ref[...] = x_ref[...] + _dot(merged.astype(BF16), wout_ref[...])


def _ffn_kernel(x_ref, p_ref, gffn_ref, wup_ref, wconv_ref, bconv_ref, wdown_ref,
                gple_ref, wpg_ref, wpp_ref, gfin_ref, o_ref,
                h_buf, act_buf, acc_buf, up_buf, *, final_norm):
    s = pl.program_id(1)

    @pl.when(s == 0)
    def _():
        up_buf[0:SHORT_HALO, :] = jnp.zeros((SHORT_HALO, 2 * D_FF), F32)

    x = x_ref[...]
    h_buf[...] = _rms_norm(x, gffn_ref[...]).astype(BF16)

    base = SHORT_HALO - (FFN_K - 1)

    def up_chunk(j):
        parts = []
        for part in range(2):
            c0 = part * D_FF + j * FFN_COLS
            cols = slice(c0, c0 + FFN_COLS)
            up_buf[SHORT_HALO:SHORT_HALO + TM, cols] = _dot(h_buf[...], wup_ref[:, cols])
            conv = up_buf[base:base + TM, cols] * wconv_ref[0:1, cols]
            for k in range(1, FFN_K):
                conv = conv + up_buf[base + k:base + k + TM, cols] * wconv_ref[k:k + 1, cols]
            up_buf[0:SHORT_HALO, cols] = up_buf[TM:TM + SHORT_HALO, cols]
            parts.append(conv + bconv_ref[:, cols])
        gate, val = parts
        act_buf[:, j * FFN_COLS:(j + 1) * FFN_COLS] = (
            gate * jax.nn.sigmoid(gate) * val).astype(BF16)

    def down_group(first, last):
        rows = slice(first * FFN_COLS, last * FFN_COLS)
        partial = _dot(act_buf[:, rows], wdown_ref[rows, :])
        acc_buf[...] = (x if first == 0 else acc_buf[...]) + partial

    n_chunks = D_FF // FFN_COLS
    groups = [(a, min(a + DOWN_GROUP, n_chunks)) for a in range(0, n_chunks, DOWN_GROUP)]
    for gi, (first, last) in enumerate(groups):
        for j in range(first, last):
            up_chunk(j)
        if gi > 0:
            down_group(*groups[gi - 1])
    down_group(*groups[-1])

    proj = _dot(p_ref[...].astype(BF16), wpp_ref[...])
    y = acc_buf[...]
    h2 = _rms_norm(y, gple_ref[...]).astype(BF16)
    gate = jax.nn.sigmoid(_dot(h2, wpg_ref[...]))
    y = y + gate * proj
    if final_norm:
        y = _rms_norm(y, gfin_ref[...])
    o_ref[...] = y


def _const_spec(shape, layer):
    nd = len(shape)
    return pl.BlockSpec((None,) + tuple(shape), lambda b, s: (layer,) + (0,) * nd,
                        pipeline_mode=pl.Buffered(1))


def _tile_spec(width):
    return pl.BlockSpec((None, TM, width), lambda b, s: (b, s, 0))


def _compiler_params():
    return pltpu.CompilerParams(dimension_semantics=("arbitrary", "arbitrary"),
                                vmem_limit_bytes=VMEM_LIMIT_BYTES)


def _mixer_call(x, layer, params):
    batch, seq, _ = x.shape
    in_specs = [_tile_spec(D_MODEL)] + [_const_spec(p.shape[1:], layer) for p in params]
    return pl.pallas_call(
        _mixer_kernel,
        grid=(batch, seq // TM),
        in_specs=in_specs,
        out_specs=_tile_spec(D_MODEL),
        out_shape=jax.ShapeDtypeStruct(x.shape, F32),
        scratch_shapes=[
            pltpu.VMEM((TM, D_MODEL), BF16),
            pltpu.VMEM((N_BRANCH, TM, BW), BF16),
            pltpu.VMEM((N_BRANCH, TM, D_MODEL), F32),
            pltpu.VMEM((CONF_HALO + TM, BW), F32),
            pltpu.VMEM((SHORT_HALO + TM, BW), F32),
            pltpu.VMEM((POOL_HALO + TM, BW), F32),
        ],
        compiler_params=_compiler_params(),
        name=f"mixer_l{layer}",
    )(x, *params)


def _ffn_call(x, p, layer, params, gfin, final_norm):
    batch, seq, _ = x.shape
    in_specs = ([_tile_spec(D_MODEL),
                 pl.BlockSpec((None, None, TM, PLE_DIM), lambda b, s: (layer, b, s, 0))]
                + [_const_spec(q.shape[1:], layer) for q in params]
                + [pl.BlockSpec(gfin.shape, lambda b, s: (0, 0), pipeline_mode=pl.Buffered(1))])
    return pl.pallas_call(
        functools.partial(_ffn_kernel, final_norm=final_norm),
        grid=(batch, seq // TM),
        in_specs=in_specs,
        out_specs=_tile_spec(D_MODEL),
        out_shape=jax.ShapeDtypeStruct(x.shape, F32),
        scratch_shapes=[
            pltpu.VMEM((TM, D_MODEL), BF16),
            pltpu.VMEM((TM, D_FF), BF16),
            pltpu.VMEM((TM, D_MODEL), F32),
            pltpu.VMEM((SHORT_HALO + TM, 2 * D_FF), F32),
        ],
        compiler_params=_compiler_params(),
        name=f"ffn_l{layer}",
    )(x, p, *params, gfin)


def kernel(x, p, g_mix, w_in, gmlp_ln_g, gmlp_ln_b, gmlp_w_s, gmlp_b_s, conf_w_dw, conf_b_dw, conf_ln_g, conf_ln_b, short_w, pool_w, pool_scale, w_branch, w_out, g_ffn, ffn_w_up, ffn_w_conv, ffn_b_conv, ffn_w_down, g_ple, ple_w_gate, ple_w_proj, g_final):
    depth = w_in.shape[0]
    assert x.shape[1] % TM == 0 and TM % CHUNK == 0
    row = lambda a: a.reshape(depth, 1, -1)

    pool_bd = jnp.zeros((depth, BW, BW), F32)
    for g in range(N_GROUPS):
        sl = slice(g * GROUP_DIM, (g + 1) * GROUP_DIM)
        pool_bd = pool_bd.at[:, sl, sl].set(pool_w[:, g])
    mixer_params = [
        row(g_mix),
        w_in.astype(BF16),
        row(gmlp_ln_g), row(gmlp_ln_b),
        gmlp_w_s.transpose(0, 2, 1, 3).reshape(depth, CHUNK, N_GROUPS * CHUNK).astype(BF16),
        jnp.repeat(gmlp_b_s.transpose(0, 2, 1), GROUP_DIM, axis=2),
        conf_w_dw, row(conf_b_dw), row(conf_ln_g), row(conf_ln_b),
        short_w,
        pool_bd.astype(BF16), row(pool_scale),
        w_branch.astype(BF16), w_out.astype(BF16),
    ]
    ffn_params = [
        row(g_ffn), ffn_w_up.astype(BF16), ffn_w_conv, row(ffn_b_conv),
        ffn_w_down.astype(BF16), row(g_ple), ple_w_gate.astype(BF16), ple_w_proj.astype(BF16),
    ]
    gfin = g_final.reshape(1, -1)

    for i in range(depth):
        x = _mixer_call(x, i, mixer_params)
        x = _ffn_call(x, p, i, ffn_params, gfin, final_norm=(i == depth - 1))
    return x
```

```python
import functools

import jax
import jax.numpy as jnp
from jax import lax
from jax.experimental import pallas as pl
from jax.experimental.pallas import tpu as pltpu

D_MODEL = 1024
PLE_DIM = 256
N_BRANCH = 4
BW = 256
N_GROUPS = 4
GROUP_DIM = BW // N_GROUPS
GROUP_SHIFT = GROUP_DIM.bit_length() - 1
CHUNK = 128
CONF_K = 31
SHORT_K = 3
FFN_K = 3
D_FF = 2816
POOL_WINDOWS = (2, 4, 8, 16)
EPS = 1e-6

A_OFF, B_OFF, C_OFF, D_OFF, G_OFF = 0, 2 * BW, 4 * BW, 7 * BW, 8 * BW
IN_COLS = G_OFF + N_BRANCH * D_MODEL

SUBLANES = 8
LANES = 128
MIX_TILES = BW // LANES
TM = 512
CONF_HALO = 32
SHORT_HALO = 8
POOL_HALO = 24
assert POOL_HALO - SUBLANES >= max(POOL_WINDOWS) - 2
assert LANES == 2 * GROUP_DIM and len(POOL_WINDOWS) == 2 * MIX_TILES
FFN_COLS = 256
DOWN_GROUP = 2
DOWN_LAG = 6
VMEM_LIMIT_BYTES = 56 * 1024 * 1024

BF16 = jnp.bfloat16
F32 = jnp.float32


def _dot(a, b):
    return jnp.dot(a, b, preferred_element_type=F32)


def _rms_norm(x, g):
    ms = jnp.mean(x * x, axis=-1, keepdims=True)
    return x * lax.rsqrt(ms + EPS) * g


def _mixer_kernel(x_ref, g_ref, win_ref, lng_ref, lnb_ref, ws_ref, bs_ref,
                  cw_ref, cb_ref, clg_ref, clb_ref, sw_ref, pw_ref, ps_ref,
                  wbr_ref, wout_ref, o_ref,
                  h_buf, y_buf, gate_buf, conf_buf, short_buf, pool_buf):
    s = pl.program_id(1)

    @pl.when(s == 0)
    def _():
        conf_buf[:, 0:CONF_HALO, :] = jnp.zeros((MIX_TILES, CONF_HALO, LANES), F32)
        short_buf[:, 0:SHORT_HALO, :] = jnp.zeros((MIX_TILES, SHORT_HALO, LANES), F32)
        pool_buf[:, 0:POOL_HALO, :] = jnp.zeros((MIX_TILES, POOL_HALO, LANES), F32)

    def store_tiles(buf, halo, value):
        for c in range(MIX_TILES):
            buf[c, halo:halo + TM, :] = value[:, c * LANES:(c + 1) * LANES]

    h_buf[...] = _rms_norm(x_ref[...], g_ref[...]).astype(BF16)

    zb = _dot(h_buf[...], win_ref[:, B_OFF:B_OFF + 2 * BW])
    store_tiles(conf_buf, CONF_HALO, zb[:, :BW] * jax.nn.sigmoid(zb[:, BW:]))
    zc = _dot(h_buf[...], win_ref[:, C_OFF:C_OFF + 3 * BW])
    store_tiles(short_buf, SHORT_HALO, zc[:, BW:2 * BW] * zc[:, 2 * BW:])
    zd = _dot(h_buf[...], win_ref[:, D_OFF:D_OFF + BW])
    store_tiles(pool_buf, POOL_HALO, zd)
    za = jax.nn.gelu(_dot(h_buf[...], win_ref[:, A_OFF:A_OFF + 2 * BW]))
    for k in range(N_BRANCH):
        zg = _dot(h_buf[...], win_ref[:, G_OFF + k * D_MODEL:G_OFF + (k + 1) * D_MODEL])
        gate_buf[k] = jax.nn.sigmoid(zg)

    def causal_taps(buf, halo, w_ref):
        n_taps = w_ref.shape[0]
        base = halo - (n_taps - 1)
        outs = []
        for c in range(MIX_TILES):
            lanes = slice(c * LANES, (c + 1) * LANES)
            acc = buf[c, base:base + TM, :] * w_ref[0:1, lanes]
            for k in range(1, n_taps):
                acc = acc + buf[c, base + k:base + k + TM, :] * w_ref[k:k + 1, lanes]
            buf[c, 0:halo, :] = buf[c, TM:TM + halo, :]
            outs.append(acc)
        return jnp.concatenate(outs, axis=1)

    yb = causal_taps(conf_buf, CONF_HALO, cw_ref) + cb_ref[...]

    y_buf[2] = (zc[:, :BW] * causal_taps(short_buf, SHORT_HALO, sw_ref)).astype(BF16)

    t_pos = s * TM + lax.broadcasted_iota(jnp.int32, (TM, LANES), 0) + 1
    low = lax.broadcasted_iota(jnp.int32, (TM, LANES), 1) < GROUP_DIM
    span = POOL_HALO - SUBLANES + TM
    pooled = []
    for c in range(MIX_TILES):
        w_lo, w_hi = POOL_WINDOWS[2 * c], POOL_WINDOWS[2 * c + 1]
        carry_rows = pool_buf[c, TM:TM + POOL_HALO, :]
        w = 1
        while w < w_hi:
            doubled = (pool_buf[c, SUBLANES:SUBLANES + span, :]
                       + pool_buf[c, SUBLANES - w:SUBLANES - w + span, :])
            pool_buf[c, SUBLANES:SUBLANES + span, :] = doubled
            w *= 2
            if w == w_lo:
                lo_sum = doubled[POOL_HALO - SUBLANES:, :]
        win_sum = jnp.where(low, lo_sum, doubled[POOL_HALO - SUBLANES:, :])
        cnt = jnp.minimum(t_pos, jnp.where(low, w_lo, w_hi)).astype(F32)
        pooled.append(win_sum / cnt - zd[:, c * LANES:(c + 1) * LANES])
        pool_buf[c, 0:POOL_HALO, :] = carry_rows
    pooled = jnp.concatenate(pooled, axis=1).astype(BF16)

    u = za[:, :BW]
    v = za[:, BW:]
    mu = jnp.mean(v, axis=-1, keepdims=True)
    vc = v - mu
    var = jnp.mean(vc * vc, axis=-1, keepdims=True)
    v = (vc * lax.rsqrt(var + EPS) * lng_ref[...] + lnb_ref[...]).astype(BF16)

    y_buf[3] = (_dot(pooled, pw_ref[...]) * ps_ref[...]).astype(BF16)
    row = lax.broadcasted_iota(jnp.int32, (CHUNK, N_GROUPS * CHUNK), 0)
    col = lax.broadcasted_iota(jnp.int32, (CHUNK, N_GROUPS * CHUNK), 1)
    w_cat = jnp.where((col & (CHUNK - 1)) <= row, ws_ref[...], jnp.zeros((), BF16))
    lane_group = lax.broadcasted_iota(jnp.int32, (CHUNK, BW), 1) >> GROUP_SHIFT
    for n in range(TM // CHUNK):
        vn = v[n * CHUNK:(n + 1) * CHUNK, :]
        v_bd = jnp.concatenate(
            [jnp.where(lane_group == g, vn, jnp.zeros((), BF16)) for g in range(N_GROUPS)],
            axis=0)
        mixed = _dot(w_cat, v_bd) + bs_ref[...]
        y_buf[0, n * CHUNK:(n + 1) * CHUNK, :] = (
            u[n * CHUNK:(n + 1) * CHUNK, :] * mixed).astype(BF16)
    gi = lax.broadcasted_iota(jnp.int32, (BW, BW), 0) >> GROUP_SHIFT
    gj = lax.broadcasted_iota(jnp.int32, (BW, BW), 1) >> GROUP_SHIFT
    g_avg = jnp.where(gi == gj, 1.0 / GROUP_DIM, 0.0).astype(BF16)
    ybc = yb - _dot(yb.astype(BF16), g_avg)
    gvar = _dot((ybc * ybc).astype(BF16), g_avg)
    ybn = ybc * lax.rsqrt(gvar + EPS) * clg_ref[...] + clb_ref[...]
    y_buf[1] = (ybn * jax.nn.sigmoid(ybn)).astype(BF16)

    merged = None
    for k in (2, 3, 0, 1):
        term = _dot(y_buf[k], wbr_ref[k]) * gate_buf[k]
        merged = term if merged is None else merged + term
    o_ref[...] = x_ref[...] + _dot(merged.astype(BF16), wout_ref[...])


def _ffn_kernel(x_ref, p_ref, gffn_ref, wup_ref, wconv_ref, bconv_ref, wdown_ref,
                gple_ref, wpg_ref, wpp_ref, gfin_ref, o_ref,
                h_buf, act_buf, acc_buf, up_buf, *, final_norm):
    s = pl.program_id(1)

    @pl.when(s == 0)
    def _():
        up_buf[:, 0:SHORT_HALO, :] = jnp.zeros((2 * D_FF // LANES, SHORT_HALO, LANES), F32)

    x = x_ref[...]
    h_buf[...] = _rms_norm(x, gffn_ref[...]).astype(BF16)

    base = SHORT_HALO - (FFN_K - 1)

    def up_chunk(j):
        parts = []
        for part in range(2):
            c0 = part * D_FF + j * FFN_COLS
            up = _dot(h_buf[...], wup_ref[:, c0:c0 + FFN_COLS])
            convs = []
            for c in range(FFN_COLS // LANES):
                tile = c0 // LANES + c
                lanes = slice(tile * LANES, (tile + 1) * LANES)
                up_buf[tile, SHORT_HALO:SHORT_HALO + TM, :] = up[:, c * LANES:(c + 1) * LANES]
                conv = up_buf[tile, base:base + TM, :] * wconv_ref[0:1, lanes]
                for k in range(1, FFN_K):
                    conv = conv + up_buf[tile, base + k:base + k + TM, :] * wconv_ref[k:k + 1, lanes]
                up_buf[tile, 0:SHORT_HALO, :] = up_buf[tile, TM:TM + SHORT_HALO, :]
                convs.append(conv)
            parts.append(jnp.concatenate(convs, axis=1) + bconv_ref[:, c0:c0 + FFN_COLS])
        gate, val = parts
        act_buf[:, j * FFN_COLS:(j + 1) * FFN_COLS] = (
            gate * jax.nn.sigmoid(gate) * val).astype(BF16)

    def down_group(first, last):
        rows = slice(first * FFN_COLS, last * FFN_COLS)
        partial = _dot(act_buf[:, rows], wdown_ref[rows, :])
        acc_buf[...] = (x if first == 0 else acc_buf[...]) + partial

    n_chunks = D_FF // FFN_COLS
    groups = [(a, min(a + DOWN_GROUP, n_chunks)) for a in range(0, n_chunks, DOWN_GROUP)]
    for gi, (first, last) in enumerate(groups):
        for j in range(first, last):
            up_chunk(j)
        if gi >= DOWN_LAG:
            down_group(*groups[gi - DOWN_LAG])
    for g in groups[len(groups) - DOWN_LAG:]:
        down_group(*g)

    proj = _dot(p_ref[...].astype(BF16), wpp_ref[...])
    y = acc_buf[...]
    h2 = _rms_norm(y, gple_ref[...]).astype(BF16)
    gate = jax.nn.sigmoid(_dot(h2, wpg_ref[...]))
    y = y + gate * proj
    if final_norm:
        y = _rms_norm(y, gfin_ref[...])
    o_ref[...] = y


def _const_spec(shape, layer):
    nd = len(shape)
    return pl.BlockSpec((None,) + tuple(shape), lambda b, s: (layer,) + (0,) * nd,
                        pipeline_mode=pl.Buffered(1))


def _tile_spec(width):
    return pl.BlockSpec((None, TM, width), lambda b, s: (b, s, 0))


def _compiler_params():
    return pltpu.CompilerParams(dimension_semantics=("arbitrary", "arbitrary"),
                                vmem_limit_bytes=VMEM_LIMIT_BYTES)


def _mixer_call(x, layer, params):
    batch, seq, _ = x.shape
    in_specs = [_tile_spec(D_MODEL)] + [_const_spec(p.shape[1:], layer) for p in params]
    return pl.pallas_call(
        _mixer_kernel,
        grid=(batch, seq // TM),
        in_specs=in_specs,
        out_specs=_tile_spec(D_MODEL),
        out_shape=jax.ShapeDtypeStruct(x.shape, F32),
        scratch_shapes=[
            pltpu.VMEM((TM, D_MODEL), BF16),
            pltpu.VMEM((N_BRANCH, TM, BW), BF16),
            pltpu.VMEM((N_BRANCH, TM, D_MODEL), F32),
            pltpu.VMEM((MIX_TILES, CONF_HALO + TM, LANES), F32),
            pltpu.VMEM((MIX_TILES, SHORT_HALO + TM, LANES), F32),
            pltpu.VMEM((MIX_TILES, POOL_HALO + TM, LANES), F32),
        ],
        compiler_params=_compiler_params(),
        name=f"mixer_l{layer}",
    )(x, *params)


def _ffn_call(x, p, layer, params, gfin, final_norm):
    batch, seq, _ = x.shape
    in_specs = ([_tile_spec(D_MODEL),
                 pl.BlockSpec((None, None, TM, PLE_DIM), lambda b, s: (layer, b, s, 0))]
                + [_const_spec(q.shape[1:], layer) for q in params]
                + [pl.BlockSpec(gfin.shape, lambda b, s: (0, 0), pipeline_mode=pl.Buffered(1))])
    return pl.pallas_call(
        functools.partial(_ffn_kernel, final_norm=final_norm),
        grid=(batch, seq // TM),
        in_specs=in_specs,
        out_specs=_tile_spec(D_MODEL),
        out_shape=jax.ShapeDtypeStruct(x.shape, F32),
        scratch_shapes=[
            pltpu.VMEM((TM, D_MODEL), BF16),
            pltpu.VMEM((TM, D_FF), BF16),
            pltpu.VMEM((TM, D_MODEL), F32),
            pltpu.VMEM((2 * D_FF // LANES, SHORT_HALO + TM, LANES), F32),
        ],
        compiler_params=_compiler_params(),
        name=f"ffn_l{layer}",
    )(x, p, *params, gfin)


def kernel(x, p, g_mix, w_in, gmlp_ln_g, gmlp_ln_b, gmlp_w_s, gmlp_b_s, conf_w_dw, conf_b_dw, conf_ln_g, conf_ln_b, short_w, pool_w, pool_scale, w_branch, w_out, g_ffn, ffn_w_up, ffn_w_conv, ffn_b_conv, ffn_w_down, g_ple, ple_w_gate, ple_w_proj, g_final):
    depth = w_in.shape[0]
    assert x.shape[1] % TM == 0 and TM % CHUNK == 0
    row = lambda a: a.reshape(depth, 1, -1)

    pool_bd = jnp.zeros((depth, BW, BW), F32)
    for g in range(N_GROUPS):
        sl = slice(g * GROUP_DIM, (g + 1) * GROUP_DIM)
        pool_bd = pool_bd.at[:, sl, sl].set(pool_w[:, g])
    mixer_params = [
        row(g_mix),
        w_in.astype(BF16),
        row(gmlp_ln_g), row(gmlp_ln_b),
        gmlp_w_s.transpose(0, 2, 1, 3).reshape(depth, CHUNK, N_GROUPS * CHUNK).astype(BF16),
        jnp.repeat(gmlp_b_s.transpose(0, 2, 1), GROUP_DIM, axis=2),
        conf_w_dw, row(conf_b_dw), row(conf_ln_g), row(conf_ln_b),
        short_w,
        pool_bd.astype(BF16), row(pool_scale),
        w_branch.astype(BF16), w_out.astype(BF16),
    ]
    ffn_params = [
        row(g_ffn), ffn_w_up.astype(BF16), ffn_w_conv, row(ffn_b_conv),
        ffn_w_down.astype(BF16), row(g_ple), ple_w_gate.astype(BF16), ple_w_proj.astype(BF16),
    ]
    gfin = g_final.reshape(1, -1)

    for i in range(depth):
        x = _mixer_call(x, i, mixer_params)
        x = _ffn_call(x, p, i, ffn_params, gfin, final_norm=(i == depth - 1))
    return x
```

```python
import functools
from typing import NamedTuple

import jax
import jax.numpy as jnp
from jax import lax
from jax.experimental import pallas as pl
from jax.experimental.pallas import tpu as pltpu

D_MODEL = 1024
PLE_DIM = 256
N_BRANCH = 4
BW = 256
N_GROUPS = 4
GROUP_DIM = BW // N_GROUPS
GROUP_SHIFT = GROUP_DIM.bit_length() - 1
CHUNK = 128
CONF_K = 31
SHORT_K = 3
FFN_K = 3
D_FF = 2816
POOL_WINDOWS = (2, 4, 8, 16)
EPS = 1e-6

A_OFF, B_OFF, C_OFF, D_OFF, G_OFF = 0, 2 * BW, 4 * BW, 7 * BW, 8 * BW
IN_COLS = G_OFF + N_BRANCH * D_MODEL

SUBLANES = 8
LANES = 128
MIX_TILES = BW // LANES
TM = 512
CONF_HALO = 32
SHORT_HALO = 8
POOL_HALO = 24
assert POOL_HALO - SUBLANES >= max(POOL_WINDOWS) - 2
assert LANES == 2 * GROUP_DIM and len(POOL_WINDOWS) == 2 * MIX_TILES
FFN_COLS = 256
DOWN_GROUP = 2
DOWN_LAG = 6
STREAM_ROWS = 256
WIDE_STREAM_ROWS = 128
VMEM_LIMIT_BYTES = 56 * 1024 * 1024

BF16 = jnp.bfloat16
F32 = jnp.float32


def _dot(a, b):
    return jnp.dot(a, b, preferred_element_type=F32)


def _rms_norm(x, g):
    ms = jnp.mean(x * x, axis=-1, keepdims=True)
    return x * lax.rsqrt(ms + EPS) * g


def _stream_to_bf16(src, dst, rows):
    n_chunks = src.shape[0] // rows
    assert n_chunks * rows == src.shape[0]

    def scoped(stage, sem):
        def copy(i, slot):
            return pltpu.make_async_copy(src.at[pl.ds(i * rows, rows), :], stage.at[slot],
                                         sem.at[slot])

        copy(0, 0).start()

        def body(i, carry):
            slot = lax.rem(i, 2)

            @pl.when(i + 1 < n_chunks)
            def _():
                copy(i + 1, 1 - slot).start()

            copy(i, slot).wait()
            dst[pl.ds(pl.multiple_of(i * rows, rows), rows), :] = stage[slot].astype(BF16)
            return carry

        lax.fori_loop(0, n_chunks, body, 0)

    pl.run_scoped(scoped, pltpu.VMEM((2, rows, src.shape[1]), F32),
                  pltpu.SemaphoreType.DMA((2,)))


def _first_step():
    return (pl.program_id(0) == 0) & (pl.program_id(1) == 0)


def _mixer_kernel(x_ref, g_ref, win_hbm, lng_ref, lnb_ref, ws_ref, bs_ref,
                  cw_ref, cb_ref, clg_ref, clb_ref, sw_ref, pw_ref, ps_ref,
                  wbr_hbm, wout_hbm, o_ref,
                  win_ref, wbr_ref, wout_ref,
                  h_buf, y_buf, gate_buf, conf_buf, short_buf, pool_buf, *, layer):
    s = pl.program_id(1)

    @pl.when(_first_step())
    def _():
        _stream_to_bf16(win_hbm.at[layer], win_ref, WIDE_STREAM_ROWS)
        _stream_to_bf16(wbr_hbm.at[layer], wbr_ref, STREAM_ROWS)
        _stream_to_bf16(wout_hbm.at[layer], wout_ref, STREAM_ROWS)

    @pl.when(s == 0)
    def _():
        conf_buf[:, 0:CONF_HALO, :] = jnp.zeros((MIX_TILES, CONF_HALO, LANES), F32)
        short_buf[:, 0:SHORT_HALO, :] = jnp.zeros((MIX_TILES, SHORT_HALO, LANES), F32)
        pool_buf[:, 0:POOL_HALO, :] = jnp.zeros((MIX_TILES, POOL_HALO, LANES), F32)

    def store_tiles(buf, halo, value):
        for c in range(MIX_TILES):
            buf[c, halo:halo + TM, :] = value[:, c * LANES:(c + 1) * LANES]

    h_buf[...] = _rms_norm(x_ref[...], g_ref[...]).astype(BF16)

    zb = _dot(h_buf[...], win_ref[:, B_OFF:B_OFF + 2 * BW])
    store_tiles(conf_buf, CONF_HALO, zb[:, :BW] * jax.nn.sigmoid(zb[:, BW:]))
    zc = _dot(h_buf[...], win_ref[:, C_OFF:C_OFF + 3 * BW])
    store_tiles(short_buf, SHORT_HALO, zc[:, BW:2 * BW] * zc[:, 2 * BW:])
    zd = _dot(h_buf[...], win_ref[:, D_OFF:D_OFF + BW])
    store_tiles(pool_buf, POOL_HALO, zd)
    za = jax.nn.gelu(_dot(h_buf[...], win_ref[:, A_OFF:A_OFF + 2 * BW]))
    for k in range(N_BRANCH):
        zg = _dot(h_buf[...], win_ref[:, G_OFF + k * D_MODEL:G_OFF + (k + 1) * D_MODEL])
        gate_buf[k] = jax.nn.sigmoid(zg)

    def causal_taps(buf, halo, w_ref):
        n_taps = w_ref.shape[0]
        base = halo - (n_taps - 1)
        outs = []
        for c in range(MIX_TILES):
            lanes = slice(c * LANES, (c + 1) * LANES)
            acc = buf[c, base:base + TM, :] * w_ref[0:1, lanes]
            for k in range(1, n_taps):
                acc = acc + buf[c, base + k:base + k + TM, :] * w_ref[k:k + 1, lanes]
            buf[c, 0:halo, :] = buf[c, TM:TM + halo, :]
            outs.append(acc)
        return jnp.concatenate(outs, axis=1)

    yb = causal_taps(conf_buf, CONF_HALO, cw_ref) + cb_ref[...]

    y_buf[2] = (zc[:, :BW] * causal_taps(short_buf, SHORT_HALO, sw_ref)).astype(BF16)

    t_pos = s * TM + lax.broadcasted_iota(jnp.int32, (TM, LANES), 0) + 1
    low = lax.broadcasted_iota(jnp.int32, (TM, LANES), 1) < GROUP_DIM
    span = POOL_HALO - SUBLANES + TM
    pooled = []
    for c in range(MIX_TILES):
        w_lo, w_hi = POOL_WINDOWS[2 * c], POOL_WINDOWS[2 * c + 1]
        carry_rows = pool_buf[c, TM:TM + POOL_HALO, :]
        w = 1
        while w < w_hi:
            doubled = (pool_buf[c, SUBLANES:SUBLANES + span, :]
                       + pool_buf[c, SUBLANES - w:SUBLANES - w + span, :])
            pool_buf[c, SUBLANES:SUBLANES + span, :] = doubled
            w *= 2
            if w == w_lo:
                lo_sum = doubled[POOL_HALO - SUBLANES:, :]
        win_sum = jnp.where(low, lo_sum, doubled[POOL_HALO - SUBLANES:, :])
        cnt = jnp.minimum(t_pos, jnp.where(low, w_lo, w_hi)).astype(F32)
        pooled.append(win_sum / cnt - zd[:, c * LANES:(c + 1) * LANES])
        pool_buf[c, 0:POOL_HALO, :] = carry_rows
    pooled = jnp.concatenate(pooled, axis=1).astype(BF16)

    u = za[:, :BW]
    v = za[:, BW:]
    mu = jnp.mean(v, axis=-1, keepdims=True)
    vc = v - mu
    var = jnp.mean(vc * vc, axis=-1, keepdims=True)
    v = (vc * lax.rsqrt(var + EPS) * lng_ref[...] + lnb_ref[...]).astype(BF16)

    y_buf[3] = (_dot(pooled, pw_ref[...]) * ps_ref[...]).astype(BF16)
    row = lax.broadcasted_iota(jnp.int32, (CHUNK, N_GROUPS * CHUNK), 0)
    col = lax.broadcasted_iota(jnp.int32, (CHUNK, N_GROUPS * CHUNK), 1)
    w_cat = jnp.where((col & (CHUNK - 1)) <= row, ws_ref[...], jnp.zeros((), BF16))
    lane_group = lax.broadcasted_iota(jnp.int32, (CHUNK, BW), 1) >> GROUP_SHIFT
    for n in range(TM // CHUNK):
        vn = v[n * CHUNK:(n + 1) * CHUNK, :]
        v_bd = jnp.concatenate(
            [jnp.where(lane_group == g, vn, jnp.zeros((), BF16)) for g in range(N_GROUPS)],
            axis=0)
        mixed = _dot(w_cat, v_bd) + bs_ref[...]
        y_buf[0, n * CHUNK:(n + 1) * CHUNK, :] = (
            u[n * CHUNK:(n + 1) * CHUNK, :] * mixed).astype(BF16)
    gi = lax.broadcasted_iota(jnp.int32, (BW, BW), 0) >> GROUP_SHIFT
    gj = lax.broadcasted_iota(jnp.int32, (BW, BW), 1) >> GROUP_SHIFT
    g_avg = jnp.where(gi == gj, 1.0 / GROUP_DIM, 0.0).astype(BF16)
    ybc = yb - _dot(yb.astype(BF16), g_avg)
    gvar = _dot((ybc * ybc).astype(BF16), g_avg)
    ybn = ybc * lax.rsqrt(gvar + EPS) * clg_ref[...] + clb_ref[...]
    y_buf[1] = (ybn * jax.nn.sigmoid(ybn)).astype(BF16)

    merged = None
    for k in (2, 3, 0, 1):
        term = _dot(y_buf[k], wbr_ref[k * BW:(k + 1) * BW, :]) * gate_buf[k]
        merged = term if merged is None else merged + term
    o_ref[...] = x_ref[...] + _dot(merged.astype(BF16), wout_ref[...])


def _ffn_kernel(x_ref, p_ref, gffn_ref, wup_hbm, wconv_ref, bconv_ref, wdown_hbm,
                gple_ref, wpg_hbm, wpp_hbm, gfin_ref, o_ref,
                wup_ref, wdown_ref, wpg_ref, wpp_ref,
                h_buf, act_buf, acc_buf, up_buf, *, layer, final_norm):
    s = pl.program_id(1)

    @pl.when(_first_step())
    def _():
        _stream_to_bf16(wup_hbm.at[layer], wup_ref, WIDE_STREAM_ROWS)
        _stream_to_bf16(wdown_hbm.at[layer], wdown_ref, STREAM_ROWS)
        _stream_to_bf16(wpg_hbm.at[layer], wpg_ref, STREAM_ROWS)
        _stream_to_bf16(wpp_hbm.at[layer], wpp_ref, STREAM_ROWS)

    @pl.when(s == 0)
    def _():
        up_buf[:, 0:SHORT_HALO, :] = jnp.zeros((2 * D_FF // LANES, SHORT_HALO, LANES), F32)

    x = x_ref[...]
    h_buf[...] = _rms_norm(x, gffn_ref[...]).astype(BF16)

    base = SHORT_HALO - (FFN_K - 1)

    def up_chunk(j):
        parts = []
        for part in range(2):
            c0 = part * D_FF + j * FFN_COLS
            up = _dot(h_buf[...], wup_ref[:, c0:c0 + FFN_COLS])
            convs = []
            for c in range(FFN_COLS // LANES):
                tile = c0 // LANES + c
                lanes = slice(tile * LANES, (tile + 1) * LANES)
                up_buf[tile, SHORT_HALO:SHORT_HALO + TM, :] = up[:, c * LANES:(c + 1) * LANES]
                conv = up_buf[tile, base:base + TM, :] * wconv_ref[0:1, lanes]
                for k in range(1, FFN_K):
                    conv = conv + up_buf[tile, base + k:base + k + TM, :] * wconv_ref[k:k + 1, lanes]
                up_buf[tile, 0:SHORT_HALO, :] = up_buf[tile, TM:TM + SHORT_HALO, :]
                convs.append(conv)
            parts.append(jnp.concatenate(convs, axis=1) + bconv_ref[:, c0:c0 + FFN_COLS])
        gate, val = parts
        act_buf[:, j * FFN_COLS:(j + 1) * FFN_COLS] = (
            gate * jax.nn.sigmoid(gate) * val).astype(BF16)

    def down_group(first, last):
        rows = slice(first * FFN_COLS, last * FFN_COLS)
        partial = _dot(act_buf[:, rows], wdown_ref[rows, :])
        acc_buf[...] = (x if first == 0 else acc_buf[...]) + partial

    n_chunks = D_FF // FFN_COLS
    groups = [(a, min(a + DOWN_GROUP, n_chunks)) for a in range(0, n_chunks, DOWN_GROUP)]
    for gi, (first, last) in enumerate(groups):
        for j in range(first, last):
            up_chunk(j)
        if gi >= DOWN_LAG:
            down_group(*groups[gi - DOWN_LAG])
    for g in groups[len(groups) - DOWN_LAG:]:
        down_group(*g)

    proj = _dot(p_ref[...].astype(BF16), wpp_ref[...])
    y = acc_buf[...]
    h2 = _rms_norm(y, gple_ref[...]).astype(BF16)
    gate = jax.nn.sigmoid(_dot(h2, wpg_ref[...]))
    y = y + gate * proj
    if final_norm:
        y = _rms_norm(y, gfin_ref[...])
    o_ref[...] = y


def _const_spec(shape, layer):
    nd = len(shape)
    return pl.BlockSpec((None,) + tuple(shape), lambda b, s: (layer,) + (0,) * nd,
                        pipeline_mode=pl.Buffered(1))


def _tile_spec(width):
    return pl.BlockSpec((None, TM, width), lambda b, s: (b, s, 0))


def _compiler_params():
    return pltpu.CompilerParams(dimension_semantics=("arbitrary", "arbitrary"),
                                vmem_limit_bytes=VMEM_LIMIT_BYTES)


class _Streamed(NamedTuple):
    array: jax.Array


def _param_specs(params, layer):
    specs, weight_bufs, operands = [], [], []
    for q in params:
        if isinstance(q, _Streamed):
            specs.append(pl.BlockSpec(memory_space=pl.ANY))
            weight_bufs.append(pltpu.VMEM(q.array.shape[1:], BF16))
            operands.append(q.array)
        else:
            specs.append(_const_spec(q.shape[1:], layer))
            operands.append(q)
    return specs, weight_bufs, operands


def _mixer_call(x, layer, params):
    batch, seq, _ = x.shape
    param_specs, weight_bufs, operands = _param_specs(params, layer)
    return pl.pallas_call(
        functools.partial(_mixer_kernel, layer=layer),
        grid=(batch, seq // TM),
        in_specs=[_tile_spec(D_MODEL)] + param_specs,
        out_specs=_tile_spec(D_MODEL),
        out_shape=jax.ShapeDtypeStruct(x.shape, F32),
        scratch_shapes=weight_bufs + [
            pltpu.VMEM((TM, D_MODEL), BF16),
            pltpu.VMEM((N_BRANCH, TM, BW), BF16),
            pltpu.VMEM((N_BRANCH, TM, D_MODEL), F32),
            pltpu.VMEM((MIX_TILES, CONF_HALO + TM, LANES), F32),
            pltpu.VMEM((MIX_TILES, SHORT_HALO + TM, LANES), F32),
            pltpu.VMEM((MIX_TILES, POOL_HALO + TM, LANES), F32),
        ],
        compiler_params=_compiler_params(),
        name=f"mixer_l{layer}",
    )(x, *operands)


def _ffn_call(x, p, layer, params, gfin, final_norm):
    batch, seq, _ = x.shape
    param_specs, weight_bufs, operands = _param_specs(params, layer)
    in_specs = ([_tile_spec(D_MODEL),
                 pl.BlockSpec((None, None, TM, PLE_DIM), lambda b, s: (layer, b, s, 0))]
                + param_specs
                + [pl.BlockSpec(gfin.shape, lambda b, s: (0, 0), pipeline_mode=pl.Buffered(1))])
    return pl.pallas_call(
        functools.partial(_ffn_kernel, layer=layer, final_norm=final_norm),
        grid=(batch, seq // TM),
        in_specs=in_specs,
        out_specs=_tile_spec(D_MODEL),
        out_shape=jax.ShapeDtypeStruct(x.shape, F32),
        scratch_shapes=weight_bufs + [
            pltpu.VMEM((TM, D_MODEL), BF16),
            pltpu.VMEM((TM, D_FF), BF16),
            pltpu.VMEM((TM, D_MODEL), F32),
            pltpu.VMEM((2 * D_FF // LANES, SHORT_HALO + TM, LANES), F32),
        ],
        compiler_params=_compiler_params(),
        name=f"ffn_l{layer}",
    )(x, p, *operands, gfin)


def kernel(x, p, g_mix, w_in, gmlp_ln_g, gmlp_ln_b, gmlp_w_s, gmlp_b_s, conf_w_dw, conf_b_dw, conf_ln_g, conf_ln_b, short_w, pool_w, pool_scale, w_branch, w_out, g_ffn, ffn_w_up, ffn_w_conv, ffn_b_conv, ffn_w_down, g_ple, ple_w_gate, ple_w_proj, g_final):
    depth = w_in.shape[0]
    assert x.shape[1] % TM == 0 and TM % CHUNK == 0
    row = lambda a: a.reshape(depth, 1, -1)

    pool_bd = jnp.zeros((depth, BW, BW), F32)
    for g in range(N_GROUPS):
        sl = slice(g * GROUP_DIM, (g + 1) * GROUP_DIM)
        pool_bd = pool_bd.at[:, sl, sl].set(pool_w[:, g])
    mixer_params = [
        row(g_mix),
        _Streamed(w_in),
        row(gmlp_ln_g), row(gmlp_ln_b),
        gmlp_w_s.transpose(0, 2, 1, 3).reshape(depth, CHUNK, N_GROUPS * CHUNK).astype(BF16),
        jnp.repeat(gmlp_b_s.transpose(0, 2, 1), GROUP_DIM, axis=2),
        conf_w_dw, row(conf_b_dw), row(conf_ln_g), row(conf_ln_b),
        short_w,
        pool_bd.astype(BF16), row(pool_scale),
        _Streamed(w_branch.reshape(depth, N_BRANCH * BW, D_MODEL)), _Streamed(w_out),
    ]
    ffn_params = [
        row(g_ffn), _Streamed(ffn_w_up), ffn_w_conv, row(ffn_b_conv),
        _Streamed(ffn_w_down), row(g_ple), _Streamed(ple_w_gate), _Streamed(ple_w_proj),
    ]
    gfin = g_final.reshape(1, -1)

    for i in range(depth):
        x = _mixer_call(x, i, mixer_params)
        x = _ffn_call(x, p, i, ffn_params, gfin, final_norm=(i == depth - 1))
    return x
```

```python
import functools
from typing import NamedTuple

import jax
import jax.numpy as jnp
from jax import lax
from jax.experimental import pallas as pl
from jax.experimental.pallas import tpu as pltpu

D_MODEL = 1024
PLE_DIM = 256
N_BRANCH = 4
BW = 256
N_GROUPS = 4
GROUP_DIM = BW // N_GROUPS
GROUP_SHIFT = GROUP_DIM.bit_length() - 1
CHUNK = 128
CONF_K = 31
SHORT_K = 3
FFN_K = 3
D_FF = 2816
POOL_WINDOWS = (2, 4, 8, 16)
EPS = 1e-6

A_OFF, B_OFF, C_OFF, D_OFF, G_OFF = 0, 2 * BW, 4 * BW, 7 * BW, 8 * BW
IN_COLS = G_OFF + N_BRANCH * D_MODEL

SUBLANES = 8
LANES = 128
MIX_TILES = BW // LANES
TM = 512
CONF_HALO = 32
SHORT_HALO = 8
POOL_HALO = 24
assert POOL_HALO - SUBLANES >= max(POOL_WINDOWS) - 2
assert LANES == 2 * GROUP_DIM and len(POOL_WINDOWS) == 2 * MIX_TILES
FFN_COLS = 256
DOWN_GROUP = 2
DOWN_LAG = 6
STREAM_ROWS = 256
WIDE_STREAM_ROWS = 128
VMEM_LIMIT_BYTES = 56 * 1024 * 1024

BF16 = jnp.bfloat16
F32 = jnp.float32


def _dot(a, b):
    return jnp.dot(a, b, preferred_element_type=F32)


def _rms_norm(x, g):
    ms = jnp.mean(x * x, axis=-1, keepdims=True)
    return x * lax.rsqrt(ms + EPS) * g


def _stream_to_bf16(src, dst, rows):
    n_chunks = src.shape[0] // rows
    half = rows // 2
    assert n_chunks * rows == src.shape[0] and 2 * half == rows

    def scoped(stage, sem):
        def copies(i, slot):
            return [pltpu.make_async_copy(src.at[pl.ds(i * rows + part * half, half), :],
                                          stage.at[slot, pl.ds(part * half, half), :],
                                          sem.at[slot, part])
                    for part in range(2)]

        def start(i, slot):
            for part, cp in enumerate(copies(i, slot)):
                cp.start(priority=part)

        start(0, 0)

        def body(i, carry):
            slot = lax.rem(i, 2)

            @pl.when(i + 1 < n_chunks)
            def _():
                start(i + 1, 1 - slot)

            for cp in copies(i, slot):
                cp.wait()
            dst[pl.ds(pl.multiple_of(i * rows, rows), rows), :] = stage[slot].astype(BF16)
            return carry

        lax.fori_loop(0, n_chunks, body, 0)

    pl.run_scoped(scoped, pltpu.VMEM((2, rows, src.shape[1]), F32),
                  pltpu.SemaphoreType.DMA((2, 2)))


def _first_step():
    return (pl.program_id(0) == 0) & (pl.program_id(1) == 0)


def _mixer_kernel(x_ref, g_ref, win_hbm, lng_ref, lnb_ref, ws_ref, bs_ref,
                  cw_ref, cb_ref, clg_ref, clb_ref, sw_ref, pw_ref, ps_ref,
                  wbr_hbm, wout_hbm, o_ref,
                  win_ref, wbr_ref, wout_ref,
                  h_buf, y_buf, gate_buf, conf_buf, short_buf, pool_buf, *, layer):
    s = pl.program_id(1)

    @pl.when(_first_step())
    def _():
        _stream_to_bf16(win_hbm.at[layer], win_ref, WIDE_STREAM_ROWS)
        _stream_to_bf16(wbr_hbm.at[layer], wbr_ref, STREAM_ROWS)
        _stream_to_bf16(wout_hbm.at[layer], wout_ref, STREAM_ROWS)

    @pl.when(s == 0)
    def _():
        conf_buf[:, 0:CONF_HALO, :] = jnp.zeros((MIX_TILES, CONF_HALO, LANES), F32)
        short_buf[:, 0:SHORT_HALO, :] = jnp.zeros((MIX_TILES, SHORT_HALO, LANES), F32)
        pool_buf[:, 0:POOL_HALO, :] = jnp.zeros((MIX_TILES, POOL_HALO, LANES), F32)

    def store_tiles(buf, halo, value):
        for c in range(MIX_TILES):
            buf[c, halo:halo + TM, :] = value[:, c * LANES:(c + 1) * LANES]

    h_buf[...] = _rms_norm(x_ref[...], g_ref[...]).astype(BF16)

    zb = _dot(h_buf[...], win_ref[:, B_OFF:B_OFF + 2 * BW])
    store_tiles(conf_buf, CONF_HALO, zb[:, :BW] * jax.nn.sigmoid(zb[:, BW:]))
    zc = _dot(h_buf[...], win_ref[:, C_OFF:C_OFF + 3 * BW])
    store_tiles(short_buf, SHORT_HALO, zc[:, BW:2 * BW] * zc[:, 2 * BW:])
    zd = _dot(h_buf[...], win_ref[:, D_OFF:D_OFF + BW])
    store_tiles(pool_buf, POOL_HALO, zd)
    za = jax.nn.gelu(_dot(h_buf[...], win_ref[:, A_OFF:A_OFF + 2 * BW]))
    for k in range(N_BRANCH):
        zg = _dot(h_buf[...], win_ref[:, G_OFF + k * D_MODEL:G_OFF + (k + 1) * D_MODEL])
        gate_buf[k] = jax.nn.sigmoid(zg)

    def causal_taps(buf, halo, w_ref):
        n_taps = w_ref.shape[0]
        base = halo - (n_taps - 1)
        outs = []
        for c in range(MIX_TILES):
            lanes = slice(c * LANES, (c + 1) * LANES)
            acc = buf[c, base:base + TM, :] * w_ref[0:1, lanes]
            for k in range(1, n_taps):
                acc = acc + buf[c, base + k:base + k + TM, :] * w_ref[k:k + 1, lanes]
            buf[c, 0:halo, :] = buf[c, TM:TM + halo, :]
            outs.append(acc)
        return jnp.concatenate(outs, axis=1)

    yb = causal_taps(conf_buf, CONF_HALO, cw_ref) + cb_ref[...]

    y_buf[2] = (zc[:, :BW] * causal_taps(short_buf, SHORT_HALO, sw_ref)).astype(BF16)

    t_pos = s * TM + lax.broadcasted_iota(jnp.int32, (TM, LANES), 0) + 1
    low = lax.broadcasted_iota(jnp.int32, (TM, LANES), 1) < GROUP_DIM
    span = POOL_HALO - SUBLANES + TM
    pooled = []
    for c in range(MIX_TILES):
        w_lo, w_hi = POOL_WINDOWS[2 * c], POOL_WINDOWS[2 * c + 1]
        carry_rows = pool_buf[c, TM:TM + POOL_HALO, :]
        w = 1
        while w < w_hi:
            doubled = (pool_buf[c, SUBLANES:SUBLANES + span, :]
                       + pool_buf[c, SUBLANES - w:SUBLANES - w + span, :])
            pool_buf[c, SUBLANES:SUBLANES + span, :] = doubled
            w *= 2
            if w == w_lo:
                lo_sum = doubled[POOL_HALO - SUBLANES:, :]
        win_sum = jnp.where(low, lo_sum, doubled[POOL_HALO - SUBLANES:, :])
        cnt = jnp.minimum(t_pos, jnp.where(low, w_lo, w_hi)).astype(F32)
        pooled.append(win_sum / cnt - zd[:, c * LANES:(c + 1) * LANES])
        pool_buf[c, 0:POOL_HALO, :] = carry_rows
    pooled = jnp.concatenate(pooled, axis=1).astype(BF16)

    u = za[:, :BW]
    v = za[:, BW:]
    mu = jnp.mean(v, axis=-1, keepdims=True)
    vc = v - mu
    var = jnp.mean(vc * vc, axis=-1, keepdims=True)
    v = (vc * lax.rsqrt(var + EPS) * lng_ref[...] + lnb_ref[...]).astype(BF16)

    y_buf[3] = (_dot(pooled, pw_ref[...]) * ps_ref[...]).astype(BF16)
    row = lax.broadcasted_iota(jnp.int32, (CHUNK, N_GROUPS * CHUNK), 0)
    col = lax.broadcasted_iota(jnp.int32, (CHUNK, N_GROUPS * CHUNK), 1)
    w_cat = jnp.where((col & (CHUNK - 1)) <= row, ws_ref[...], jnp.zeros((), BF16))
    lane_group = lax.broadcasted_iota(jnp.int32, (CHUNK, BW), 1) >> GROUP_SHIFT
    for n in range(TM // CHUNK):
        vn = v[n * CHUNK:(n + 1) * CHUNK, :]
        v_bd = jnp.concatenate(
            [jnp.where(lane_group == g, vn, jnp.zeros((), BF16)) for g in range(N_GROUPS)],
            axis=0)
        mixed = _dot(w_cat, v_bd) + bs_ref[...]
        y_buf[0, n * CHUNK:(n + 1) * CHUNK, :] = (
            u[n * CHUNK:(n + 1) * CHUNK, :] * mixed).astype(BF16)
    gi = lax.broadcasted_iota(jnp.int32, (BW, BW), 0) >> GROUP_SHIFT
    gj = lax.broadcasted_iota(jnp.int32, (BW, BW), 1) >> GROUP_SHIFT
    g_avg = jnp.where(gi == gj, 1.0 / GROUP_DIM, 0.0).astype(BF16)
    ybc = yb - _dot(yb.astype(BF16), g_avg)
    gvar = _dot((ybc * ybc).astype(BF16), g_avg)
    ybn = ybc * lax.rsqrt(gvar + EPS) * clg_ref[...] + clb_ref[...]
    y_buf[1] = (ybn * jax.nn.sigmoid(ybn)).astype(BF16)

    merged = None
    for k in (2, 3, 0, 1):
        term = _dot(y_buf[k], wbr_ref[k * BW:(k + 1) * BW, :]) * gate_buf[k]
        merged = term if merged is None else merged + term
    o_ref[...] = x_ref[...] + _dot(merged.astype(BF16), wout_ref[...])


def _ffn_kernel(x_ref, p_ref, gffn_ref, wup_hbm, wconv_ref, bconv_ref, wdown_hbm,
                gple_ref, wpg_hbm, wpp_hbm, gfin_ref, o_ref,
                wup_ref, wdown_ref, wpg_ref, wpp_ref,
                h_buf, act_buf, acc_buf, up_buf, *, layer, final_norm):
    s = pl.program_id(1)

    @pl.when(_first_step())
    def _():
        _stream_to_bf16(wup_hbm.at[layer], wup_ref, WIDE_STREAM_ROWS)
        _stream_to_bf16(wdown_hbm.at[layer], wdown_ref, STREAM_ROWS)
        _stream_to_bf16(wpg_hbm.at[layer], wpg_ref, STREAM_ROWS)
        _stream_to_bf16(wpp_hbm.at[layer], wpp_ref, STREAM_ROWS)

    @pl.when(s == 0)
    def _():
        up_buf[:, 0:SHORT_HALO, :] = jnp.zeros((2 * D_FF // LANES, SHORT_HALO, LANES), F32)

    x = x_ref[...]
    h_buf[...] = _rms_norm(x, gffn_ref[...]).astype(BF16)

    base = SHORT_HALO - (FFN_K - 1)

    def up_chunk(j):
        parts = []
        for part in range(2):
            c0 = part * D_FF + j * FFN_COLS
            up = _dot(h_buf[...], wup_ref[:, c0:c0 + FFN_COLS])
            convs = []
            for c in range(FFN_COLS // LANES):
                tile = c0 // LANES + c
                lanes = slice(tile * LANES, (tile + 1) * LANES)
                up_buf[tile, SHORT_HALO:SHORT_HALO + TM, :] = up[:, c * LANES:(c + 1) * LANES]
                conv = up_buf[tile, base:base + TM, :] * wconv_ref[0:1, lanes]
                for k in range(1, FFN_K):
                    conv = conv + up_buf[tile, base + k:base + k + TM, :] * wconv_ref[k:k + 1, lanes]
                up_buf[tile, 0:SHORT_HALO, :] = up_buf[tile, TM:TM + SHORT_HALO, :]
                convs.append(conv)
            parts.append(jnp.concatenate(convs, axis=1) + bconv_ref[:, c0:c0 + FFN_COLS])
        gate, val = parts
        act_buf[:, j * FFN_COLS:(j + 1) * FFN_COLS] = (
            gate * jax.nn.sigmoid(gate) * val).astype(BF16)

    def down_group(first, last):
        rows = slice(first * FFN_COLS, last * FFN_COLS)
        partial = _dot(act_buf[:, rows], wdown_ref[rows, :])
        acc_buf[...] = (x if first == 0 else acc_buf[...]) + partial

    n_chunks = D_FF // FFN_COLS
    groups = [(a, min(a + DOWN_GROUP, n_chunks)) for a in range(0, n_chunks, DOWN_GROUP)]
    for gi, (first, last) in enumerate(groups):
        for j in range(first, last):
            up_chunk(j)
        if gi >= DOWN_LAG:
            down_group(*groups[gi - DOWN_LAG])
    for g in groups[len(groups) - DOWN_LAG:]:
        down_group(*g)

    proj = _dot(p_ref[...].astype(BF16), wpp_ref[...])
    y = acc_buf[...]
    h2 = _rms_norm(y, gple_ref[...]).astype(BF16)
    gate = jax.nn.sigmoid(_dot(h2, wpg_ref[...]))
    y = y + gate * proj
    if final_norm:
        y = _rms_norm(y, gfin_ref[...])
    o_ref[...] = y


def _const_spec(shape, layer):
    nd = len(shape)
    return pl.BlockSpec((None,) + tuple(shape), lambda b, s: (layer,) + (0,) * nd,
                        pipeline_mode=pl.Buffered(1))


def _tile_spec(width):
    return pl.BlockSpec((None, TM, width), lambda b, s: (b, s, 0))


def _compiler_params():
    return pltpu.CompilerParams(dimension_semantics=("arbitrary", "arbitrary"),
                                vmem_limit_bytes=VMEM_LIMIT_BYTES)


class _Streamed(NamedTuple):
    array: jax.Array


def _param_specs(params, layer):
    specs, weight_bufs, operands = [], [], []
    for q in params:
        if isinstance(q, _Streamed):
            specs.append(pl.BlockSpec(memory_space=pl.ANY))
            weight_bufs.append(pltpu.VMEM(q.array.shape[1:], BF16))
            operands.append(q.array)
        else:
            specs.append(_const_spec(q.shape[1:], layer))
            operands.append(q)
    return specs, weight_bufs, operands


def _mixer_call(x, layer, params):
    batch, seq, _ = x.shape
    param_specs, weight_bufs, operands = _param_specs(params, layer)
    return pl.pallas_call(
        functools.partial(_mixer_kernel, layer=layer),
        grid=(batch, seq // TM),
        in_specs=[_tile_spec(D_MODEL)] + param_specs,
        out_specs=_tile_spec(D_MODEL),
        out_shape=jax.ShapeDtypeStruct(x.shape, F32),
        scratch_shapes=weight_bufs + [
            pltpu.VMEM((TM, D_MODEL), BF16),
            pltpu.VMEM((N_BRANCH, TM, BW), BF16),
            pltpu.VMEM((N_BRANCH, TM, D_MODEL), F32),
            pltpu.VMEM((MIX_TILES, CONF_HALO + TM, LANES), F32),
            pltpu.VMEM((MIX_TILES, SHORT_HALO + TM, LANES), F32),
            pltpu.VMEM((MIX_TILES, POOL_HALO + TM, LANES), F32),
        ],
        compiler_params=_compiler_params(),
        name=f"mixer_l{layer}",
    )(x, *operands)


def _ffn_call(x, p, layer, params, gfin, final_norm):
    batch, seq, _ = x.shape
    param_specs, weight_bufs, operands = _param_specs(params, layer)
    in_specs = ([_tile_spec(D_MODEL),
                 pl.BlockSpec((None, None, TM, PLE_DIM), lambda b, s: (layer, b, s, 0))]
                + param_specs
                + [pl.BlockSpec(gfin.shape, lambda b, s: (0, 0), pipeline_mode=pl.Buffered(1))])
    return pl.pallas_call(
        functools.partial(_ffn_kernel, layer=layer, final_norm=final_norm),
        grid=(batch, seq // TM),
        in_specs=in_specs,
        out_specs=_tile_spec(D_MODEL),
        out_shape=jax.ShapeDtypeStruct(x.shape, F32),
        scratch_shapes=weight_bufs + [
            pltpu.VMEM((TM, D_MODEL), BF16),
            pltpu.VMEM((TM, D_FF), BF16),
            pltpu.VMEM((TM, D_MODEL), F32),
            pltpu.VMEM((2 * D_FF // LANES, SHORT_HALO + TM, LANES), F32),
        ],
        compiler_params=_compiler_params(),
        name=f"ffn_l{layer}",
    )(x, p, *operands, gfin)


def kernel(x, p, g_mix, w_in, gmlp_ln_g, gmlp_ln_b, gmlp_w_s, gmlp_b_s, conf_w_dw, conf_b_dw, conf_ln_g, conf_ln_b, short_w, pool_w, pool_scale, w_branch, w_out, g_ffn, ffn_w_up, ffn_w_conv, ffn_b_conv, ffn_w_down, g_ple, ple_w_gate, ple_w_proj, g_final):
    depth = w_in.shape[0]
    assert x.shape[1] % TM == 0 and TM % CHUNK == 0
    row = lambda a: a.reshape(depth, 1, -1)

    pool_bd = jnp.zeros((depth, BW, BW), F32)
    for g in range(N_GROUPS):
        sl = slice(g * GROUP_DIM, (g + 1) * GROUP_DIM)
        pool_bd = pool_bd.at[:, sl, sl].set(pool_w[:, g])
    mixer_params = [
        row(g_mix),
        _Streamed(w_in),
        row(gmlp_ln_g), row(gmlp_ln_b),
        gmlp_w_s.transpose(0, 2, 1, 3).reshape(depth, CHUNK, N_GROUPS * CHUNK).astype(BF16),
        jnp.repeat(gmlp_b_s.transpose(0, 2, 1), GROUP_DIM, axis=2),
        conf_w_dw, row(conf_b_dw), row(conf_ln_g), row(conf_ln_b),
        short_w,
        pool_bd.astype(BF16), row(pool_scale),
        _Streamed(w_branch.reshape(depth, N_BRANCH * BW, D_MODEL)), _Streamed(w_out),
    ]
    ffn_params = [
        row(g_ffn), _Streamed(ffn_w_up), ffn_w_conv, row(ffn_b_conv),
        _Streamed(ffn_w_down), row(g_ple), _Streamed(ple_w_gate), _Streamed(ple_w_proj),
    ]
    gfin = g_final.reshape(1, -1)

    for i in range(depth):
        x = _mixer_call(x, i, mixer_params)
        x = _ffn_call(x, p, i, ffn_params, gfin, final_norm=(i == depth - 1))
    return x
```

```python
import functools
from typing import NamedTuple

import jax
import jax.numpy as jnp
from jax import lax
from jax.experimental import pallas as pl
from jax.experimental.pallas import tpu as pltpu

D_MODEL = 1024
PLE_DIM = 256
N_BRANCH = 4
BW = 256
N_GROUPS = 4
GROUP_DIM = BW // N_GROUPS
GROUP_SHIFT = GROUP_DIM.bit_length() - 1
CHUNK = 128
CONF_K = 31
SHORT_K = 3
FFN_K = 3
D_FF = 2816
POOL_WINDOWS = (2, 4, 8, 16)
EPS = 1e-6

A_OFF, B_OFF, C_OFF, D_OFF, G_OFF = 0, 2 * BW, 4 * BW, 7 * BW, 8 * BW
IN_COLS = G_OFF + N_BRANCH * D_MODEL

SUBLANES = 8
LANES = 128
MIX_TILES = BW // LANES
TM_MIX = 512
TM_FFN = 512
GATE_DTYPE = jnp.float32
CONF_HALO = 32
SHORT_HALO = 8
POOL_HALO = 24
assert POOL_HALO - SUBLANES >= max(POOL_WINDOWS) - 2
assert LANES == 2 * GROUP_DIM and len(POOL_WINDOWS) == 2 * MIX_TILES
FFN_COLS = 256
DOWN_GROUP = 2
DOWN_LAG = 6
STREAM_ROWS = 256
WIDE_STREAM_ROWS = 128
VMEM_LIMIT_BYTES = 56 * 1024 * 1024

BF16 = jnp.bfloat16
F32 = jnp.float32


def _dot(a, b):
    return jnp.dot(a, b, preferred_element_type=F32)


def _rms_norm(x, g):
    ms = jnp.mean(x * x, axis=-1, keepdims=True)
    return x * lax.rsqrt(ms + EPS) * g


def _stream_to_bf16(src, dst, rows):
    n_chunks = src.shape[0] // rows
    assert n_chunks * rows == src.shape[0]

    def scoped(stage, sem):
        def copy(i, slot):
            return pltpu.make_async_copy(src.at[pl.ds(i * rows, rows), :], stage.at[slot],
                                         sem.at[slot])

        copy(0, 0).start()

        def body(i, carry):
            slot = lax.rem(i, 2)

            @pl.when(i + 1 < n_chunks)
            def _():
                copy(i + 1, 1 - slot).start()

            copy(i, slot).wait()
            dst[pl.ds(pl.multiple_of(i * rows, rows), rows), :] = stage[slot].astype(BF16)
            return carry

        lax.fori_loop(0, n_chunks, body, 0)

    pl.run_scoped(scoped, pltpu.VMEM((2, rows, src.shape[1]), F32),
                  pltpu.SemaphoreType.DMA((2,)))


def _first_step():
    return (pl.program_id(0) == 0) & (pl.program_id(1) == 0)


def _mixer_kernel(x_ref, g_ref, win_hbm, lng_ref, lnb_ref, ws_ref, bs_ref,
                  cw_ref, cb_ref, clg_ref, clb_ref, sw_ref, pw_ref, ps_ref,
                  wbr_hbm, wout_hbm, o_ref,
                  win_ref, wbr_ref, wout_ref,
                  h_buf, y_buf, gate_buf, conf_buf, short_buf, pool_buf, *, layer):
    s = pl.program_id(1)
    tm = x_ref.shape[0]

    @pl.when(_first_step())
    def _():
        _stream_to_bf16(win_hbm.at[layer], win_ref, WIDE_STREAM_ROWS)
        _stream_to_bf16(wbr_hbm.at[layer], wbr_ref, STREAM_ROWS)
        _stream_to_bf16(wout_hbm.at[layer], wout_ref, STREAM_ROWS)

    @pl.when(s == 0)
    def _():
        conf_buf[:, 0:CONF_HALO, :] = jnp.zeros((MIX_TILES, CONF_HALO, LANES), F32)
        short_buf[:, 0:SHORT_HALO, :] = jnp.zeros((MIX_TILES, SHORT_HALO, LANES), F32)
        pool_buf[:, 0:POOL_HALO, :] = jnp.zeros((MIX_TILES, POOL_HALO, LANES), F32)

    def store_tiles(buf, halo, value):
        for c in range(MIX_TILES):
            buf[c, halo:halo + tm, :] = value[:, c * LANES:(c + 1) * LANES]

    h_buf[...] = _rms_norm(x_ref[...], g_ref[...]).astype(BF16)

    z = _dot(h_buf[...], win_ref[:, 0:G_OFF])
    zb = z[:, B_OFF:B_OFF + 2 * BW]
    store_tiles(conf_buf, CONF_HALO, zb[:, :BW] * jax.nn.sigmoid(zb[:, BW:]))
    zc = z[:, C_OFF:C_OFF + 3 * BW]
    store_tiles(short_buf, SHORT_HALO, zc[:, BW:2 * BW] * zc[:, 2 * BW:])
    zd = z[:, D_OFF:D_OFF + BW]
    store_tiles(pool_buf, POOL_HALO, zd)
    za = jax.nn.gelu(z[:, A_OFF:A_OFF + 2 * BW])

    def merge_gate(k):
        zg = _dot(h_buf[...], win_ref[:, G_OFF + k * D_MODEL:G_OFF + (k + 1) * D_MODEL])
        gate_buf[k] = jax.nn.sigmoid(zg).astype(gate_buf.dtype)

    merge_gate(0)
    merge_gate(1)

    def causal_taps(buf, halo, w_ref):
        n_taps = w_ref.shape[0]
        base = halo - (n_taps - 1)
        outs = []
        for c in range(MIX_TILES):
            lanes = slice(c * LANES, (c + 1) * LANES)
            acc = buf[c, base:base + tm, :] * w_ref[0:1, lanes]
            for k in range(1, n_taps):
                acc = acc + buf[c, base + k:base + k + tm, :] * w_ref[k:k + 1, lanes]
            buf[c, 0:halo, :] = buf[c, tm:tm + halo, :]
            outs.append(acc)
        return jnp.concatenate(outs, axis=1)

    yb = causal_taps(conf_buf, CONF_HALO, cw_ref) + cb_ref[...]

    y_buf[2] = (zc[:, :BW] * causal_taps(short_buf, SHORT_HALO, sw_ref)).astype(BF16)

    t_pos = s * tm + lax.broadcasted_iota(jnp.int32, (tm, LANES), 0) + 1
    low = lax.broadcasted_iota(jnp.int32, (tm, LANES), 1) < GROUP_DIM
    span = POOL_HALO - SUBLANES + tm
    pooled = []
    for c in range(MIX_TILES):
        w_lo, w_hi = POOL_WINDOWS[2 * c], POOL_WINDOWS[2 * c + 1]
        carry_rows = pool_buf[c, tm:tm + POOL_HALO, :]
        w = 1
        while w < w_hi:
            doubled = (pool_buf[c, SUBLANES:SUBLANES + span, :]
                       + pool_buf[c, SUBLANES - w:SUBLANES - w + span, :])
            pool_buf[c, SUBLANES:SUBLANES + span, :] = doubled
            w *= 2
            if w == w_lo:
                lo_sum = doubled[POOL_HALO - SUBLANES:, :]
        win_sum = jnp.where(low, lo_sum, doubled[POOL_HALO - SUBLANES:, :])
        cnt = jnp.minimum(t_pos, jnp.where(low, w_lo, w_hi)).astype(F32)
        pooled.append(win_sum / cnt - zd[:, c * LANES:(c + 1) * LANES])
        pool_buf[c, 0:POOL_HALO, :] = carry_rows
    pooled = jnp.concatenate(pooled, axis=1).astype(BF16)

    y_buf[3] = (_dot(pooled, pw_ref[...]) * ps_ref[...]).astype(BF16)
    merge_gate(2)
    gi = lax.broadcasted_iota(jnp.int32, (BW, BW), 0) >> GROUP_SHIFT
    gj = lax.broadcasted_iota(jnp.int32, (BW, BW), 1) >> GROUP_SHIFT
    g_avg = jnp.where(gi == gj, 1.0 / GROUP_DIM, 0.0).astype(BF16)
    ybc = yb - _dot(yb.astype(BF16), g_avg)
    merge_gate(3)

    u = za[:, :BW]
    v = za[:, BW:]
    mu = jnp.mean(v, axis=-1, keepdims=True)
    vc = v - mu
    var = jnp.mean(vc * vc, axis=-1, keepdims=True)
    v = (vc * lax.rsqrt(var + EPS) * lng_ref[...] + lnb_ref[...]).astype(BF16)
    row = lax.broadcasted_iota(jnp.int32, (CHUNK, N_GROUPS * CHUNK), 0)
    col = lax.broadcasted_iota(jnp.int32, (CHUNK, N_GROUPS * CHUNK), 1)
    w_cat = jnp.where((col & (CHUNK - 1)) <= row, ws_ref[...], jnp.zeros((), BF16))
    lane_group = lax.broadcasted_iota(jnp.int32, (CHUNK, BW), 1) >> GROUP_SHIFT
    for n in range(tm // CHUNK):
        vn = v[n * CHUNK:(n + 1) * CHUNK, :]
        v_bd = jnp.concatenate(
            [jnp.where(lane_group == g, vn, jnp.zeros((), BF16)) for g in range(N_GROUPS)],
            axis=0)
        mixed = _dot(w_cat, v_bd) + bs_ref[...]
        y_buf[0, n * CHUNK:(n + 1) * CHUNK, :] = (
            u[n * CHUNK:(n + 1) * CHUNK, :] * mixed).astype(BF16)
    gvar = _dot((ybc * ybc).astype(BF16), g_avg)
    ybn = ybc * lax.rsqrt(gvar + EPS) * clg_ref[...] + clb_ref[...]
    y_buf[1] = (ybn * jax.nn.sigmoid(ybn)).astype(BF16)

    merged = None
    for k in (2, 3, 0, 1):
        term = _dot(y_buf[k], wbr_ref[k * BW:(k + 1) * BW, :]) * gate_buf[k].astype(F32)
        merged = term if merged is None else merged + term
    o_ref[...] = x_ref[...] + _dot(merged.astype(BF16), wout_ref[...])


def _ffn_kernel(x_ref, p_ref, gffn_ref, wup_hbm, wconv_ref, bconv_ref, wdown_hbm,
                gple_ref, wpg_hbm, wpp_hbm, gfin_ref, o_ref,
                wup_ref, wdown_ref, wpg_ref, wpp_ref,
                h_buf, act_buf, acc_buf, up_buf, *, layer, final_norm):
    s = pl.program_id(1)
    tm = x_ref.shape[0]

    @pl.when(_first_step())
    def _():
        _stream_to_bf16(wup_hbm.at[layer], wup_ref, WIDE_STREAM_ROWS)
        _stream_to_bf16(wdown_hbm.at[layer], wdown_ref, STREAM_ROWS)
        _stream_to_bf16(wpg_hbm.at[layer], wpg_ref, STREAM_ROWS)
        _stream_to_bf16(wpp_hbm.at[layer], wpp_ref, STREAM_ROWS)

    @pl.when(s == 0)
    def _():
        up_buf[:, 0:SHORT_HALO, :] = jnp.zeros((2 * D_FF // LANES, SHORT_HALO, LANES), F32)

    x = x_ref[...]
    h_buf[...] = _rms_norm(x, gffn_ref[...]).astype(BF16)

    base = SHORT_HALO - (FFN_K - 1)

    def up_chunk(j):
        parts = []
        for part in range(2):
            c0 = part * D_FF + j * FFN_COLS
            up = _dot(h_buf[...], wup_ref[:, c0:c0 + FFN_COLS])
            convs = []
            for c in range(FFN_COLS // LANES):
                tile = c0 // LANES + c
                lanes = slice(tile * LANES, (tile + 1) * LANES)
                up_buf[tile, SHORT_HALO:SHORT_HALO + tm, :] = up[:, c * LANES:(c + 1) * LANES]
                conv = up_buf[tile, base:base + tm, :] * wconv_ref[0:1, lanes]
                for k in range(1, FFN_K):
                    conv = conv + up_buf[tile, base + k:base + k + tm, :] * wconv_ref[k:k + 1, lanes]
                up_buf[tile, 0:SHORT_HALO, :] = up_buf[tile, tm:tm + SHORT_HALO, :]
                convs.append(conv)
            parts.append(jnp.concatenate(convs, axis=1) + bconv_ref[:, c0:c0 + FFN_COLS])
        gate, val = parts
        act_buf[:, j * FFN_COLS:(j + 1) * FFN_COLS] = (
            gate * jax.nn.sigmoid(gate) * val).astype(BF16)

    def down_group(first, last):
        rows = slice(first * FFN_COLS, last * FFN_COLS)
        partial = _dot(act_buf[:, rows], wdown_ref[rows, :])
        acc_buf[...] = (x if first == 0 else acc_buf[...]) + partial

    n_chunks = D_FF // FFN_COLS
    groups = [(a, min(a + DOWN_GROUP, n_chunks)) for a in range(0, n_chunks, DOWN_GROUP)]
    for gi, (first, last) in enumerate(groups):
        for j in range(first, last):
            up_chunk(j)
        if gi >= DOWN_LAG:
            down_group(*groups[gi - DOWN_LAG])
    for g in groups[len(groups) - DOWN_LAG:]:
        down_group(*g)

    proj = _dot(p_ref[...].astype(BF16), wpp_ref[...])
    y = acc_buf[...]
    h2 = _rms_norm(y, gple_ref[...]).astype(BF16)
    gate = jax.nn.sigmoid(_dot(h2, wpg_ref[...]))
    y = y + gate * proj
    if final_norm:
        y = _rms_norm(y, gfin_ref[...])
    o_ref[...] = y


def _const_spec(shape, layer):
    nd = len(shape)
    return pl.BlockSpec((None,) + tuple(shape), lambda b, s: (layer,) + (0,) * nd,
                        pipeline_mode=pl.Buffered(1))


def _tile_spec(tm, width):
    return pl.BlockSpec((None, tm, width), lambda b, s: (b, s, 0))


def _compiler_params():
    return pltpu.CompilerParams(dimension_semantics=("arbitrary", "arbitrary"),
                                vmem_limit_bytes=VMEM_LIMIT_BYTES)


class _Streamed(NamedTuple):
    array: jax.Array


def _param_specs(params, layer):
    specs, weight_bufs, operands = [], [], []
    for q in params:
        if isinstance(q, _Streamed):
            specs.append(pl.BlockSpec(memory_space=pl.ANY))
            weight_bufs.append(pltpu.VMEM(q.array.shape[1:], BF16))
            operands.append(q.array)
        else:
            specs.append(_const_spec(q.shape[1:], layer))
            operands.append(q)
    return specs, weight_bufs, operands


def _mixer_call(x, layer, params):
    batch, seq, _ = x.shape
    param_specs, weight_bufs, operands = _param_specs(params, layer)
    return pl.pallas_call(
        functools.partial(_mixer_kernel, layer=layer),
        grid=(batch, seq // TM_MIX),
        in_specs=[_tile_spec(TM_MIX, D_MODEL)] + param_specs,
        out_specs=_tile_spec(TM_MIX, D_MODEL),
        out_shape=jax.ShapeDtypeStruct(x.shape, F32),
        scratch_shapes=weight_bufs + [
            pltpu.VMEM((TM_MIX, D_MODEL), BF16),
            pltpu.VMEM((N_BRANCH, TM_MIX, BW), BF16),
            pltpu.VMEM((N_BRANCH, TM_MIX, D_MODEL), GATE_DTYPE),
            pltpu.VMEM((MIX_TILES, CONF_HALO + TM_MIX, LANES), F32),
            pltpu.VMEM((MIX_TILES, SHORT_HALO + TM_MIX, LANES), F32),
            pltpu.VMEM((MIX_TILES, POOL_HALO + TM_MIX, LANES), F32),
        ],
        compiler_params=_compiler_params(),
        name=f"mixer_l{layer}",
    )(x, *operands)


def _ffn_call(x, p, layer, params, gfin, final_norm):
    batch, seq, _ = x.shape
    param_specs, weight_bufs, operands = _param_specs(params, layer)
    in_specs = ([_tile_spec(TM_FFN, D_MODEL),
                 pl.BlockSpec((None, None, TM_FFN, PLE_DIM), lambda b, s: (layer, b, s, 0))]
                + param_specs
                + [pl.BlockSpec(gfin.shape, lambda b, s: (0, 0), pipeline_mode=pl.Buffered(1))])
    return pl.pallas_call(
        functools.partial(_ffn_kernel, layer=layer, final_norm=final_norm),
        grid=(batch, seq // TM_FFN),
        in_specs=in_specs,
        out_specs=_tile_spec(TM_FFN, D_MODEL),
        out_shape=jax.ShapeDtypeStruct(x.shape, F32),
        scratch_shapes=weight_bufs + [
            pltpu.VMEM((TM_FFN, D_MODEL), BF16),
            pltpu.VMEM((TM_FFN, D_FF), BF16),
            pltpu.VMEM((TM_FFN, D_MODEL), F32),
            pltpu.VMEM((2 * D_FF // LANES, SHORT_HALO + TM_FFN, LANES), F32),
        ],
        compiler_params=_compiler_params(),
        name=f"ffn_l{layer}",
    )(x, p, *operands, gfin)


def kernel(x, p, g_mix, w_in, gmlp_ln_g, gmlp_ln_b, gmlp_w_s, gmlp_b_s, conf_w_dw, conf_b_dw, conf_ln_g, conf_ln_b, short_w, pool_w, pool_scale, w_branch, w_out, g_ffn, ffn_w_up, ffn_w_conv, ffn_b_conv, ffn_w_down, g_ple, ple_w_gate, ple_w_proj, g_final):
    depth = w_in.shape[0]
    assert x.shape[1] % TM_MIX == 0 and TM_MIX % CHUNK == 0 and x.shape[1] % TM_FFN == 0
    row = lambda a: a.reshape(depth, 1, -1)

    pool_bd = jnp.zeros((depth, BW, BW), F32)
    for g in range(N_GROUPS):
        sl = slice(g * GROUP_DIM, (g + 1) * GROUP_DIM)
        pool_bd = pool_bd.at[:, sl, sl].set(pool_w[:, g])
    mixer_params = [
        row(g_mix),
        _Streamed(w_in),
        row(gmlp_ln_g), row(gmlp_ln_b),
        gmlp_w_s.transpose(0, 2, 1, 3).reshape(depth, CHUNK, N_GROUPS * CHUNK).astype(BF16),
        jnp.repeat(gmlp_b_s.transpose(0, 2, 1), GROUP_DIM, axis=2),
        conf_w_dw, row(conf_b_dw), row(conf_ln_g), row(conf_ln_b),
        short_w,
        pool_bd.astype(BF16), row(pool_scale),
        _Streamed(w_branch.reshape(depth, N_BRANCH * BW, D_MODEL)), _Streamed(w_out),
    ]
    ffn_params = [
        row(g_ffn), _Streamed(ffn_w_up), ffn_w_conv, row(ffn_b_conv),
        _Streamed(ffn_w_down), row(g_ple), _Streamed(ple_w_gate), _Streamed(ple_w_proj),
    ]
    gfin = g_final.reshape(1, -1)

    for i in range(depth):
        x = _mixer_call(x, i, mixer_params)
        x = _ffn_call(x, p, i, ffn_params, gfin, final_norm=(i == depth - 1))
    return x
```

```python
import functools
from typing import NamedTuple

import jax
import jax.numpy as jnp
from jax import lax
from jax.experimental import pallas as pl
from jax.experimental.pallas import tpu as pltpu

D_MODEL = 1024
PLE_DIM = 256
N_BRANCH = 4
BW = 256
N_GROUPS = 4
GROUP_DIM = BW // N_GROUPS
GROUP_SHIFT = GROUP_DIM.bit_length() - 1
CHUNK = 128
CONF_K = 31
SHORT_K = 3
FFN_K = 3
D_FF = 2816
POOL_WINDOWS = (2, 4, 8, 16)
EPS = 1e-6

A_OFF, B_OFF, C_OFF, D_OFF, G_OFF = 0, 2 * BW, 4 * BW, 7 * BW, 8 * BW
IN_COLS = G_OFF + N_BRANCH * D_MODEL

SUBLANES = 8
LANES = 128
MIX_TILES = BW // LANES
TM_MIX = 512
TM_FFN = 512
GATE_DTYPE = jnp.float32
CONF_HALO = 32
SHORT_HALO = 8
POOL_HALO = 24
assert POOL_HALO - SUBLANES >= max(POOL_WINDOWS) - 2
assert LANES == 2 * GROUP_DIM and len(POOL_WINDOWS) == 2 * MIX_TILES
FFN_COLS = 256
DOWN_GROUP = 2
DOWN_LAG = 6
PLE_ROW_BLOCKS = 4
STREAM_ROWS = 256
WIDE_STREAM_ROWS = 128
VMEM_LIMIT_BYTES = 56 * 1024 * 1024

BF16 = jnp.bfloat16
F32 = jnp.float32


def _dot(a, b):
    return jnp.dot(a, b, preferred_element_type=F32)


def _rms_norm(x, g):
    ms = jnp.mean(x * x, axis=-1, keepdims=True)
    return x * lax.rsqrt(ms + EPS) * g


def _stream_to_bf16(src, dst, rows):
    n_chunks = src.shape[0] // rows
    assert n_chunks * rows == src.shape[0]

    def scoped(stage, sem):
        def copy(i, slot):
            return pltpu.make_async_copy(src.at[pl.ds(i * rows, rows), :], stage.at[slot],
                                         sem.at[slot])

        copy(0, 0).start()

        def body(i, carry):
            slot = lax.rem(i, 2)

            @pl.when(i + 1 < n_chunks)
            def _():
                copy(i + 1, 1 - slot).start()

            copy(i, slot).wait()
            dst[pl.ds(pl.multiple_of(i * rows, rows), rows), :] = stage[slot].astype(BF16)
            return carry

        lax.fori_loop(0, n_chunks, body, 0)

    pl.run_scoped(scoped, pltpu.VMEM((2, rows, src.shape[1]), F32),
                  pltpu.SemaphoreType.DMA((2,)))


def _first_step():
    return (pl.program_id(0) == 0) & (pl.program_id(1) == 0)


def _mixer_kernel(x_ref, g_ref, win_hbm, lng_ref, lnb_ref, ws_ref, bs_ref,
                  cw_ref, cb_ref, clg_ref, clb_ref, sw_ref, pw_ref, ps_ref,
                  wbr_hbm, wout_hbm, o_ref,
                  win_ref, wbr_ref, wout_ref,
                  h_buf, y_buf, gate_buf, conf_buf, short_buf, pool_buf, *, layer):
    s = pl.program_id(1)
    tm = x_ref.shape[0]

    @pl.when(_first_step())
    def _():
        _stream_to_bf16(win_hbm.at[layer], win_ref, WIDE_STREAM_ROWS)
        _stream_to_bf16(wbr_hbm.at[layer], wbr_ref, STREAM_ROWS)
        _stream_to_bf16(wout_hbm.at[layer], wout_ref, STREAM_ROWS)

    @pl.when(s == 0)
    def _():
        conf_buf[:, 0:CONF_HALO, :] = jnp.zeros((MIX_TILES, CONF_HALO, LANES), F32)
        short_buf[:, 0:SHORT_HALO, :] = jnp.zeros((MIX_TILES, SHORT_HALO, LANES), F32)
        pool_buf[:, 0:POOL_HALO, :] = jnp.zeros((MIX_TILES, POOL_HALO, LANES), F32)

    def store_tiles(buf, halo, value):
        for c in range(MIX_TILES):
            buf[c, halo:halo + tm, :] = value[:, c * LANES:(c + 1) * LANES]

    h_buf[...] = _rms_norm(x_ref[...], g_ref[...]).astype(BF16)

    z = _dot(h_buf[...], win_ref[:, 0:G_OFF])
    zb = z[:, B_OFF:B_OFF + 2 * BW]
    store_tiles(conf_buf, CONF_HALO, zb[:, :BW] * jax.nn.sigmoid(zb[:, BW:]))
    zc = z[:, C_OFF:C_OFF + 3 * BW]
    store_tiles(short_buf, SHORT_HALO, zc[:, BW:2 * BW] * zc[:, 2 * BW:])
    zd = z[:, D_OFF:D_OFF + BW]
    store_tiles(pool_buf, POOL_HALO, zd)
    za = jax.nn.gelu(z[:, A_OFF:A_OFF + 2 * BW])

    def merge_gate(k):
        zg = _dot(h_buf[...], win_ref[:, G_OFF + k * D_MODEL:G_OFF + (k + 1) * D_MODEL])
        gate_buf[k] = jax.nn.sigmoid(zg).astype(gate_buf.dtype)

    merge_gate(0)
    merge_gate(1)

    def causal_taps(buf, halo, w_ref):
        n_taps = w_ref.shape[0]
        base = halo - (n_taps - 1)
        outs = []
        for c in range(MIX_TILES):
            lanes = slice(c * LANES, (c + 1) * LANES)
            acc = buf[c, base:base + tm, :] * w_ref[0:1, lanes]
            for k in range(1, n_taps):
                acc = acc + buf[c, base + k:base + k + tm, :] * w_ref[k:k + 1, lanes]
            buf[c, 0:halo, :] = buf[c, tm:tm + halo, :]
            outs.append(acc)
        return jnp.concatenate(outs, axis=1)

    yb = causal_taps(conf_buf, CONF_HALO, cw_ref) + cb_ref[...]

    y_buf[2] = (zc[:, :BW] * causal_taps(short_buf, SHORT_HALO, sw_ref)).astype(BF16)

    t_pos = s * tm + lax.broadcasted_iota(jnp.int32, (tm, LANES), 0) + 1
    low = lax.broadcasted_iota(jnp.int32, (tm, LANES), 1) < GROUP_DIM
    span = POOL_HALO - SUBLANES + tm
    pooled = []
    for c in range(MIX_TILES):
        w_lo, w_hi = POOL_WINDOWS[2 * c], POOL_WINDOWS[2 * c + 1]
        carry_rows = pool_buf[c, tm:tm + POOL_HALO, :]
        w = 1
        while w < w_hi:
            doubled = (pool_buf[c, SUBLANES:SUBLANES + span, :]
                       + pool_buf[c, SUBLANES - w:SUBLANES - w + span, :])
            pool_buf[c, SUBLANES:SUBLANES + span, :] = doubled
            w *= 2
            if w == w_lo:
                lo_sum = doubled[POOL_HALO - SUBLANES:, :]
        win_sum = jnp.where(low, lo_sum, doubled[POOL_HALO - SUBLANES:, :])
        cnt = jnp.minimum(t_pos, jnp.where(low, w_lo, w_hi)).astype(F32)
        pooled.append(win_sum / cnt - zd[:, c * LANES:(c + 1) * LANES])
        pool_buf[c, 0:POOL_HALO, :] = carry_rows
    pooled = jnp.concatenate(pooled, axis=1).astype(BF16)

    y_buf[3] = (_dot(pooled, pw_ref[...]) * ps_ref[...]).astype(BF16)
    merge_gate(2)
    gi = lax.broadcasted_iota(jnp.int32, (BW, BW), 0) >> GROUP_SHIFT
    gj = lax.broadcasted_iota(jnp.int32, (BW, BW), 1) >> GROUP_SHIFT
    g_avg = jnp.where(gi == gj, 1.0 / GROUP_DIM, 0.0).astype(BF16)
    ybc = yb - _dot(yb.astype(BF16), g_avg)
    merge_gate(3)

    u = za[:, :BW]
    v = za[:, BW:]
    mu = jnp.mean(v, axis=-1, keepdims=True)
    vc = v - mu
    var = jnp.mean(vc * vc, axis=-1, keepdims=True)
    v = (vc * lax.rsqrt(var + EPS) * lng_ref[...] + lnb_ref[...]).astype(BF16)
    row = lax.broadcasted_iota(jnp.int32, (CHUNK, N_GROUPS * CHUNK), 0)
    col = lax.broadcasted_iota(jnp.int32, (CHUNK, N_GROUPS * CHUNK), 1)
    w_cat = jnp.where((col & (CHUNK - 1)) <= row, ws_ref[...], jnp.zeros((), BF16))
    lane_group = lax.broadcasted_iota(jnp.int32, (CHUNK, BW), 1) >> GROUP_SHIFT
    for n in range(tm // CHUNK):
        vn = v[n * CHUNK:(n + 1) * CHUNK, :]
        v_bd = jnp.concatenate(
            [jnp.where(lane_group == g, vn, jnp.zeros((), BF16)) for g in range(N_GROUPS)],
            axis=0)
        mixed = _dot(w_cat, v_bd) + bs_ref[...]
        y_buf[0, n * CHUNK:(n + 1) * CHUNK, :] = (
            u[n * CHUNK:(n + 1) * CHUNK, :] * mixed).astype(BF16)
    gvar = _dot((ybc * ybc).astype(BF16), g_avg)
    ybn = ybc * lax.rsqrt(gvar + EPS) * clg_ref[...] + clb_ref[...]
    y_buf[1] = (ybn * jax.nn.sigmoid(ybn)).astype(BF16)

    merged = None
    for k in (2, 3, 0, 1):
        term = _dot(y_buf[k], wbr_ref[k * BW:(k + 1) * BW, :]) * gate_buf[k].astype(F32)
        merged = term if merged is None else merged + term
    o_ref[...] = x_ref[...] + _dot(merged.astype(BF16), wout_ref[...])


def _ffn_kernel(x_ref, p_ref, gffn_ref, wup_hbm, wconv_ref, bconv_ref, wdown_hbm,
                gple_ref, wpg_hbm, wpp_hbm, gfin_ref, o_ref,
                wup_ref, wdown_ref, wpg_ref, wpp_ref,
                h_buf, act_buf, acc_buf, proj_buf, up_buf, *, layer, final_norm):
    s = pl.program_id(1)
    tm = x_ref.shape[0]

    @pl.when(_first_step())
    def _():
        _stream_to_bf16(wup_hbm.at[layer], wup_ref, WIDE_STREAM_ROWS)
        _stream_to_bf16(wdown_hbm.at[layer], wdown_ref, STREAM_ROWS)
        _stream_to_bf16(wpg_hbm.at[layer], wpg_ref, STREAM_ROWS)
        _stream_to_bf16(wpp_hbm.at[layer], wpp_ref, STREAM_ROWS)

    @pl.when(s == 0)
    def _():
        up_buf[:, 0:SHORT_HALO, :] = jnp.zeros((2 * D_FF // LANES, SHORT_HALO, LANES), F32)

    x = x_ref[...]
    proj_buf[...] = _dot(p_ref[...].astype(BF16), wpp_ref[...])
    h_buf[...] = _rms_norm(x, gffn_ref[...]).astype(BF16)

    base = SHORT_HALO - (FFN_K - 1)

    def up_chunk(j):
        parts = []
        for part in range(2):
            c0 = part * D_FF + j * FFN_COLS
            up = _dot(h_buf[...], wup_ref[:, c0:c0 + FFN_COLS])
            convs = []
            for c in range(FFN_COLS // LANES):
                tile = c0 // LANES + c
                lanes = slice(tile * LANES, (tile + 1) * LANES)
                up_buf[tile, SHORT_HALO:SHORT_HALO + tm, :] = up[:, c * LANES:(c + 1) * LANES]
                conv = up_buf[tile, base:base + tm, :] * wconv_ref[0:1, lanes]
                for k in range(1, FFN_K):
                    conv = conv + up_buf[tile, base + k:base + k + tm, :] * wconv_ref[k:k + 1, lanes]
                up_buf[tile, 0:SHORT_HALO, :] = up_buf[tile, tm:tm + SHORT_HALO, :]
                convs.append(conv)
            parts.append(jnp.concatenate(convs, axis=1) + bconv_ref[:, c0:c0 + FFN_COLS])
        gate, val = parts
        act_buf[:, j * FFN_COLS:(j + 1) * FFN_COLS] = (
            gate * jax.nn.sigmoid(gate) * val).astype(BF16)

    def down_group(first, last):
        rows = slice(first * FFN_COLS, last * FFN_COLS)
        partial = _dot(act_buf[:, rows], wdown_ref[rows, :])
        acc_buf[...] = (x if first == 0 else acc_buf[...]) + partial

    n_chunks = D_FF // FFN_COLS
    groups = [(a, min(a + DOWN_GROUP, n_chunks)) for a in range(0, n_chunks, DOWN_GROUP)]
    for gi, (first, last) in enumerate(groups):
        for j in range(first, last):
            up_chunk(j)
        if gi >= DOWN_LAG:
            down_group(*groups[gi - DOWN_LAG])
    for g in groups[len(groups) - DOWN_LAG:]:
        down_group(*g)

    block = tm // PLE_ROW_BLOCKS
    for r in range(PLE_ROW_BLOCKS):
        rows = slice(r * block, (r + 1) * block)
        y = acc_buf[rows, :]
        h2 = _rms_norm(y, gple_ref[...]).astype(BF16)
        gate = jax.nn.sigmoid(_dot(h2, wpg_ref[...]))
        y = y + gate * proj_buf[rows, :]
        if final_norm:
            y = _rms_norm(y, gfin_ref[...])
        o_ref[rows, :] = y


def _const_spec(shape, layer):
    nd = len(shape)
    return pl.BlockSpec((None,) + tuple(shape), lambda b, s: (layer,) + (0,) * nd,
                        pipeline_mode=pl.Buffered(1))


def _tile_spec(tm, width):
    return pl.BlockSpec((None, tm, width), lambda b, s: (b, s, 0))


def _compiler_params():
    return pltpu.CompilerParams(dimension_semantics=("arbitrary", "arbitrary"),
                                vmem_limit_bytes=VMEM_LIMIT_BYTES)


class _Streamed(NamedTuple):
    array: jax.Array


def _param_specs(params, layer):
    specs, weight_bufs, operands = [], [], []
    for q in params:
        if isinstance(q, _Streamed):
            specs.append(pl.BlockSpec(memory_space=pl.ANY))
            weight_bufs.append(pltpu.VMEM(q.array.shape[1:], BF16))
            operands.append(q.array)
        else:
            specs.append(_const_spec(q.shape[1:], layer))
            operands.append(q)
    return specs, weight_bufs, operands


def _mixer_call(x, layer, params):
    batch, seq, _ = x.shape
    param_specs, weight_bufs, operands = _param_specs(params, layer)
    return pl.pallas_call(
        functools.partial(_mixer_kernel, layer=layer),
        grid=(batch, seq // TM_MIX),
        in_specs=[_tile_spec(TM_MIX, D_MODEL)] + param_specs,
        out_specs=_tile_spec(TM_MIX, D_MODEL),
        out_shape=jax.ShapeDtypeStruct(x.shape, F32),
        scratch_shapes=weight_bufs + [
            pltpu.VMEM((TM_MIX, D_MODEL), BF16),
            pltpu.VMEM((N_BRANCH, TM_MIX, BW), BF16),
            pltpu.VMEM((N_BRANCH, TM_MIX, D_MODEL), GATE_DTYPE),
            pltpu.VMEM((MIX_TILES, CONF_HALO + TM_MIX, LANES), F32),
            pltpu.VMEM((MIX_TILES, SHORT_HALO + TM_MIX, LANES), F32),
            pltpu.VMEM((MIX_TILES, POOL_HALO + TM_MIX, LANES), F32),
        ],
        compiler_params=_compiler_params(),
        name=f"mixer_l{layer}",
    )(x, *operands)


def _ffn_call(x, p, layer, params, gfin, final_norm):
    batch, seq, _ = x.shape
    param_specs, weight_bufs, operands = _param_specs(params, layer)
    in_specs = ([_tile_spec(TM_FFN, D_MODEL),
                 pl.BlockSpec((None, None, TM_FFN, PLE_DIM), lambda b, s: (layer, b, s, 0))]
                + param_specs
                + [pl.BlockSpec(gfin.shape, lambda b, s: (0, 0), pipeline_mode=pl.Buffered(1))])
    return pl.pallas_call(
        functools.partial(_ffn_kernel, layer=layer, final_norm=final_norm),
        grid=(batch, seq // TM_FFN),
        in_specs=in_specs,
        out_specs=_tile_spec(TM_FFN, D_MODEL),
        out_shape=jax.ShapeDtypeStruct(x.shape, F32),
        scratch_shapes=weight_bufs + [
            pltpu.VMEM((TM_FFN, D_MODEL), BF16),
            pltpu.VMEM((TM_FFN, D_FF), BF16),
            pltpu.VMEM((TM_FFN, D_MODEL), F32),
            pltpu.VMEM((TM_FFN, D_MODEL), F32),
            pltpu.VMEM((2 * D_FF // LANES, SHORT_HALO + TM_FFN, LANES), F32),
        ],
        compiler_params=_compiler_params(),
        name=f"ffn_l{layer}",
    )(x, p, *operands, gfin)


def kernel(x, p, g_mix, w_in, gmlp_ln_g, gmlp_ln_b, gmlp_w_s, gmlp_b_s, conf_w_dw, conf_b_dw, conf_ln_g, conf_ln_b, short_w, pool_w, pool_scale, w_branch, w_out, g_ffn, ffn_w_up, ffn_w_conv, ffn_b_conv, ffn_w_down, g_ple, ple_w_gate, ple_w_proj, g_final):
    depth = w_in.shape[0]
    assert x.shape[1] % TM_MIX == 0 and TM_MIX % CHUNK == 0 and x.shape[1] % TM_FFN == 0
    row = lambda a: a.reshape(depth, 1, -1)

    pool_bd = jnp.zeros((depth, BW, BW), F32)
    for g in range(N_GROUPS):
        sl = slice(g * GROUP_DIM, (g + 1) * GROUP_DIM)
        pool_bd = pool_bd.at[:, sl, sl].set(pool_w[:, g])
    mixer_params = [
        row(g_mix),
        _Streamed(w_in),
        row(gmlp_ln_g), row(gmlp_ln_b),
        gmlp_w_s.transpose(0, 2, 1, 3).reshape(depth, CHUNK, N_GROUPS * CHUNK).astype(BF16),
        jnp.repeat(gmlp_b_s.transpose(0, 2, 1), GROUP_DIM, axis=2),
        conf_w_dw, row(conf_b_dw), row(conf_ln_g), row(conf_ln_b),
        short_w,
        pool_bd.astype(BF16), row(pool_scale),
        _Streamed(w_branch.reshape(depth, N_BRANCH * BW, D_MODEL)), _Streamed(w_out),
    ]
    ffn_params = [
        row(g_ffn), _Streamed(ffn_w_up), ffn_w_conv, row(ffn_b_conv),
        _Streamed(ffn_w_down), row(g_ple), _Streamed(ple_w_gate), _Streamed(ple_w_proj),
    ]
    gfin = g_final.reshape(1, -1)

    for i in range(depth):
        x = _mixer_call(x, i, mixer_params)
        x = _ffn_call(x, p, i, ffn_params, gfin, final_norm=(i == depth - 1))
    return x
```

```python
import functools
from typing import NamedTuple

import jax
import jax.numpy as jnp
from jax import lax
from jax.experimental import pallas as pl
from jax.experimental.pallas import tpu as pltpu

D_MODEL = 1024
PLE_DIM = 256
N_BRANCH = 4
BW = 256
N_GROUPS = 4
GROUP_DIM = BW // N_GROUPS
GROUP_SHIFT = GROUP_DIM.bit_length() - 1
CHUNK = 128
CONF_K = 31
SHORT_K = 3
FFN_K = 3
D_FF = 2816
POOL_WINDOWS = (2, 4, 8, 16)
EPS = 1e-6

A_OFF, B_OFF, C_OFF, D_OFF, G_OFF = 0, 2 * BW, 4 * BW, 7 * BW, 8 * BW
IN_COLS = G_OFF + N_BRANCH * D_MODEL

SUBLANES = 8
LANES = 128
MIX_TILES = BW // LANES
TM_MIX = 512
TM_FFN = 512
GATE_DTYPE = jnp.float32
CONF_HALO = 32
SHORT_HALO = 8
POOL_HALO = 24
assert POOL_HALO - SUBLANES >= max(POOL_WINDOWS) - 2
assert LANES == 2 * GROUP_DIM and len(POOL_WINDOWS) == 2 * MIX_TILES
FFN_COLS = 256
DOWN_GROUP = 2
DOWN_LAG = 6
PLE_ROW_BLOCKS = 1
STREAM_ROWS = 256
WIDE_STREAM_ROWS = 128
VMEM_LIMIT_BYTES = 56 * 1024 * 1024

BF16 = jnp.bfloat16
F32 = jnp.float32


def _dot(a, b):
    return jnp.dot(a, b, preferred_element_type=F32)


def _rms_norm(x, g):
    ms = jnp.mean(x * x, axis=-1, keepdims=True)
    return x * lax.rsqrt(ms + EPS) * g


def _stream_to_bf16(src, dst, rows):
    n_chunks = src.shape[0] // rows
    assert n_chunks * rows == src.shape[0]

    def scoped(stage, sem):
        def copy(i, slot):
            return pltpu.make_async_copy(src.at[pl.ds(i * rows, rows), :], stage.at[slot],
                                         sem.at[slot])

        copy(0, 0).start()

        def body(i, carry):
            slot = lax.rem(i, 2)

            @pl.when(i + 1 < n_chunks)
            def _():
                copy(i + 1, 1 - slot).start()

            copy(i, slot).wait()
            dst[pl.ds(pl.multiple_of(i * rows, rows), rows), :] = stage[slot].astype(BF16)
            return carry

        lax.fori_loop(0, n_chunks, body, 0)

    pl.run_scoped(scoped, pltpu.VMEM((2, rows, src.shape[1]), F32),
                  pltpu.SemaphoreType.DMA((2,)))


def _first_step():
    return (pl.program_id(0) == 0) & (pl.program_id(1) == 0)


def _mixer_kernel(x_ref, g_ref, win_hbm, lng_ref, lnb_ref, ws_ref, bs_ref,
                  cw_ref, cb_ref, clg_ref, clb_ref, sw_ref, pw_ref, ps_ref,
                  wbr_hbm, wout_hbm, o_ref,
                  win_ref, wbr_ref, wout_ref,
                  h_buf, y_buf, gate_buf, conf_buf, short_buf, pool_buf, *, layer):
    s = pl.program_id(1)
    tm = x_ref.shape[0]

    @pl.when(_first_step())
    def _():
        _stream_to_bf16(win_hbm.at[layer], win_ref, WIDE_STREAM_ROWS)
        _stream_to_bf16(wbr_hbm.at[layer], wbr_ref, STREAM_ROWS)
        _stream_to_bf16(wout_hbm.at[layer], wout_ref, STREAM_ROWS)

    @pl.when(s == 0)
    def _():
        conf_buf[:, 0:CONF_HALO, :] = jnp.zeros((MIX_TILES, CONF_HALO, LANES), F32)
        short_buf[:, 0:SHORT_HALO, :] = jnp.zeros((MIX_TILES, SHORT_HALO, LANES), F32)
        pool_buf[:, 0:POOL_HALO, :] = jnp.zeros((MIX_TILES, POOL_HALO, LANES), F32)

    def store_tiles(buf, halo, value):
        for c in range(MIX_TILES):
            buf[c, halo:halo + tm, :] = value[:, c * LANES:(c + 1) * LANES]

    h_buf[...] = _rms_norm(x_ref[...], g_ref[...]).astype(BF16)

    z = _dot(h_buf[...], win_ref[:, 0:G_OFF])
    zb = z[:, B_OFF:B_OFF + 2 * BW]
    store_tiles(conf_buf, CONF_HALO, zb[:, :BW] * jax.nn.sigmoid(zb[:, BW:]))
    zc = z[:, C_OFF:C_OFF + 3 * BW]
    store_tiles(short_buf, SHORT_HALO, zc[:, BW:2 * BW] * zc[:, 2 * BW:])
    zd = z[:, D_OFF:D_OFF + BW]
    store_tiles(pool_buf, POOL_HALO, zd)
    za = jax.nn.gelu(z[:, A_OFF:A_OFF + 2 * BW])

    def merge_gate(k):
        zg = _dot(h_buf[...], win_ref[:, G_OFF + k * D_MODEL:G_OFF + (k + 1) * D_MODEL])
        gate_buf[k] = jax.nn.sigmoid(zg).astype(gate_buf.dtype)

    merge_gate(0)
    merge_gate(1)

    def causal_taps(buf, halo, w_ref):
        n_taps = w_ref.shape[0]
        base = halo - (n_taps - 1)
        outs = []
        for c in range(MIX_TILES):
            lanes = slice(c * LANES, (c + 1) * LANES)
            acc = buf[c, base:base + tm, :] * w_ref[0:1, lanes]
            for k in range(1, n_taps):
                acc = acc + buf[c, base + k:base + k + tm, :] * w_ref[k:k + 1, lanes]
            buf[c, 0:halo, :] = buf[c, tm:tm + halo, :]
            outs.append(acc)
        return jnp.concatenate(outs, axis=1)

    yb = causal_taps(conf_buf, CONF_HALO, cw_ref) + cb_ref[...]

    y_buf[2] = (zc[:, :BW] * causal_taps(short_buf, SHORT_HALO, sw_ref)).astype(BF16)

    t_pos = s * tm + lax.broadcasted_iota(jnp.int32, (tm, LANES), 0) + 1
    low = lax.broadcasted_iota(jnp.int32, (tm, LANES), 1) < GROUP_DIM
    span = POOL_HALO - SUBLANES + tm
    pooled = []
    for c in range(MIX_TILES):
        w_lo, w_hi = POOL_WINDOWS[2 * c], POOL_WINDOWS[2 * c + 1]
        carry_rows = pool_buf[c, tm:tm + POOL_HALO, :]
        w = 1
        while w < w_hi:
            doubled = (pool_buf[c, SUBLANES:SUBLANES + span, :]
                       + pool_buf[c, SUBLANES - w:SUBLANES - w + span, :])
            pool_buf[c, SUBLANES:SUBLANES + span, :] = doubled
            w *= 2
            if w == w_lo:
                lo_sum = doubled[POOL_HALO - SUBLANES:, :]
        win_sum = jnp.where(low, lo_sum, doubled[POOL_HALO - SUBLANES:, :])
        cnt = jnp.minimum(t_pos, jnp.where(low, w_lo, w_hi)).astype(F32)
        pooled.append(win_sum / cnt - zd[:, c * LANES:(c + 1) * LANES])
        pool_buf[c, 0:POOL_HALO, :] = carry_rows
    pooled = jnp.concatenate(pooled, axis=1).astype(BF16)

    y_buf[3] = (_dot(pooled, pw_ref[...]) * ps_ref[...]).astype(BF16)
    merge_gate(2)
    gi = lax.broadcasted_iota(jnp.int32, (BW, BW), 0) >> GROUP_SHIFT
    gj = lax.broadcasted_iota(jnp.int32, (BW, BW), 1) >> GROUP_SHIFT
    g_avg = jnp.where(gi == gj, 1.0 / GROUP_DIM, 0.0).astype(BF16)
    ybc = yb - _dot(yb.astype(BF16), g_avg)
    merge_gate(3)

    u = za[:, :BW]
    v = za[:, BW:]
    mu = jnp.mean(v, axis=-1, keepdims=True)
    vc = v - mu
    var = jnp.mean(vc * vc, axis=-1, keepdims=True)
    v = (vc * lax.rsqrt(var + EPS) * lng_ref[...] + lnb_ref[...]).astype(BF16)
    row = lax.broadcasted_iota(jnp.int32, (CHUNK, N_GROUPS * CHUNK), 0)
    col = lax.broadcasted_iota(jnp.int32, (CHUNK, N_GROUPS * CHUNK), 1)
    w_cat = jnp.where((col & (CHUNK - 1)) <= row, ws_ref[...], jnp.zeros((), BF16))
    lane_group = lax.broadcasted_iota(jnp.int32, (CHUNK, BW), 1) >> GROUP_SHIFT
    for n in range(tm // CHUNK):
        vn = v[n * CHUNK:(n + 1) * CHUNK, :]
        v_bd = jnp.concatenate(
            [jnp.where(lane_group == g, vn, jnp.zeros((), BF16)) for g in range(N_GROUPS)],
            axis=0)
        mixed = _dot(w_cat, v_bd) + bs_ref[...]
        y_buf[0, n * CHUNK:(n + 1) * CHUNK, :] = (
            u[n * CHUNK:(n + 1) * CHUNK, :] * mixed).astype(BF16)
    gvar = _dot((ybc * ybc).astype(BF16), g_avg)
    ybn = ybc * lax.rsqrt(gvar + EPS) * clg_ref[...] + clb_ref[...]
    y_buf[1] = (ybn * jax.nn.sigmoid(ybn)).astype(BF16)

    merged = None
    for k in (2, 3, 0, 1):
        term = _dot(y_buf[k], wbr_ref[k * BW:(k + 1) * BW, :]) * gate_buf[k].astype(F32)
        merged = term if merged is None else merged + term
    o_ref[...] = x_ref[...] + _dot(merged.astype(BF16), wout_ref[...])


def _ffn_kernel(x_ref, p_ref, gffn_ref, wup_hbm, wconv_ref, bconv_ref, wdown_hbm,
                gple_ref, wpg_hbm, wpp_hbm, gfin_ref, o_ref,
                wup_ref, wdown_ref, wpg_ref, wpp_ref,
                h_buf, act_buf, acc_buf, proj_buf, up_buf, *, layer, final_norm):
    s = pl.program_id(1)
    tm = x_ref.shape[0]

    @pl.when(_first_step())
    def _():
        _stream_to_bf16(wup_hbm.at[layer], wup_ref, WIDE_STREAM_ROWS)
        _stream_to_bf16(wdown_hbm.at[layer], wdown_ref, STREAM_ROWS)
        _stream_to_bf16(wpg_hbm.at[layer], wpg_ref, STREAM_ROWS)
        _stream_to_bf16(wpp_hbm.at[layer], wpp_ref, STREAM_ROWS)

    @pl.when(s == 0)
    def _():
        up_buf[:, 0:SHORT_HALO, :] = jnp.zeros((2 * D_FF // LANES, SHORT_HALO, LANES), F32)

    x = x_ref[...]
    proj_buf[...] = _dot(p_ref[...].astype(BF16), wpp_ref[...])
    h_buf[...] = _rms_norm(x, gffn_ref[...]).astype(BF16)

    base = SHORT_HALO - (FFN_K - 1)

    def up_chunk(j):
        parts = []
        for part in range(2):
            c0 = part * D_FF + j * FFN_COLS
            up = _dot(h_buf[...], wup_ref[:, c0:c0 + FFN_COLS])
            convs = []
            for c in range(FFN_COLS // LANES):
                tile = c0 // LANES + c
                lanes = slice(tile * LANES, (tile + 1) * LANES)
                up_buf[tile, SHORT_HALO:SHORT_HALO + tm, :] = up[:, c * LANES:(c + 1) * LANES]
                conv = up_buf[tile, base:base + tm, :] * wconv_ref[0:1, lanes]
                for k in range(1, FFN_K):
                    conv = conv + up_buf[tile, base + k:base + k + tm, :] * wconv_ref[k:k + 1, lanes]
                up_buf[tile, 0:SHORT_HALO, :] = up_buf[tile, tm:tm + SHORT_HALO, :]
                convs.append(conv)
            parts.append(jnp.concatenate(convs, axis=1) + bconv_ref[:, c0:c0 + FFN_COLS])
        gate, val = parts
        act_buf[:, j * FFN_COLS:(j + 1) * FFN_COLS] = (
            gate * jax.nn.sigmoid(gate) * val).astype(BF16)

    def down_group(first, last):
        rows = slice(first * FFN_COLS, last * FFN_COLS)
        partial = _dot(act_buf[:, rows], wdown_ref[rows, :])
        acc_buf[...] = (x if first == 0 else acc_buf[...]) + partial

    n_chunks = D_FF // FFN_COLS
    groups = [(a, min(a + DOWN_GROUP, n_chunks)) for a in range(0, n_chunks, DOWN_GROUP)]
    for gi, (first, last) in enumerate(groups):
        for j in range(first, last):
            up_chunk(j)
        if gi >= DOWN_LAG:
            down_group(*groups[gi - DOWN_LAG])
    for g in groups[len(groups) - DOWN_LAG:]:
        down_group(*g)

    block = tm // PLE_ROW_BLOCKS
    for r in range(PLE_ROW_BLOCKS):
        rows = slice(r * block, (r + 1) * block)
        y = acc_buf[rows, :]
        h2 = _rms_norm(y, gple_ref[...]).astype(BF16)
        gate = jax.nn.sigmoid(_dot(h2, wpg_ref[...]))
        y = y + gate * proj_buf[rows, :]
        if final_norm:
            y = _rms_norm(y, gfin_ref[...])
        o_ref[rows, :] = y


def _const_spec(shape, layer):
    nd = len(shape)
    return pl.BlockSpec((None,) + tuple(shape), lambda b, s: (layer,) + (0,) * nd,
                        pipeline_mode=pl.Buffered(1))


def _tile_spec(tm, width):
    return pl.BlockSpec((None, tm, width), lambda b, s: (b, s, 0))


def _compiler_params():
    return pltpu.CompilerParams(dimension_semantics=("arbitrary", "arbitrary"),
                                vmem_limit_bytes=VMEM_LIMIT_BYTES)


class _Streamed(NamedTuple):
    array: jax.Array


def _param_specs(params, layer):
    specs, weight_bufs, operands = [], [], []
    for q in params:
        if isinstance(q, _Streamed):
            specs.append(pl.BlockSpec(memory_space=pl.ANY))
            weight_bufs.append(pltpu.VMEM(q.array.shape[1:], BF16))
            operands.append(q.array)
        else:
            specs.append(_const_spec(q.shape[1:], layer))
            operands.append(q)
    return specs, weight_bufs, operands


def _mixer_call(x, layer, params):
    batch, seq, _ = x.shape
    param_specs, weight_bufs, operands = _param_specs(params, layer)
    return pl.pallas_call(
        functools.partial(_mixer_kernel, layer=layer),
        grid=(batch, seq // TM_MIX),
        in_specs=[_tile_spec(TM_MIX, D_MODEL)] + param_specs,
        out_specs=_tile_spec(TM_MIX, D_MODEL),
        out_shape=jax.ShapeDtypeStruct(x.shape, F32),
        scratch_shapes=weight_bufs + [
            pltpu.VMEM((TM_MIX, D_MODEL), BF16),
            pltpu.VMEM((N_BRANCH, TM_MIX, BW), BF16),
            pltpu.VMEM((N_BRANCH, TM_MIX, D_MODEL), GATE_DTYPE),
            pltpu.VMEM((MIX_TILES, CONF_HALO + TM_MIX, LANES), F32),
            pltpu.VMEM((MIX_TILES, SHORT_HALO + TM_MIX, LANES), F32),
            pltpu.VMEM((MIX_TILES, POOL_HALO + TM_MIX, LANES), F32),
        ],
        compiler_params=_compiler_params(),
        name=f"mixer_l{layer}",
    )(x, *operands)


def _ffn_call(x, p, layer, params, gfin, final_norm):
    batch, seq, _ = x.shape
    param_specs, weight_bufs, operands = _param_specs(params, layer)
    in_specs = ([_tile_spec(TM_FFN, D_MODEL),
                 pl.BlockSpec((None, None, TM_FFN, PLE_DIM), lambda b, s: (layer, b, s, 0))]
                + param_specs
                + [pl.BlockSpec(gfin.shape, lambda b, s: (0, 0), pipeline_mode=pl.Buffered(1))])
    return pl.pallas_call(
        functools.partial(_ffn_kernel, layer=layer, final_norm=final_norm),
        grid=(batch, seq // TM_FFN),
        in_specs=in_specs,
        out_specs=_tile_spec(TM_FFN, D_MODEL),
        out_shape=jax.ShapeDtypeStruct(x.shape, F32),
        scratch_shapes=weight_bufs + [
            pltpu.VMEM((TM_FFN, D_MODEL), BF16),
            pltpu.VMEM((TM_FFN, D_FF), BF16),
            pltpu.VMEM((TM_FFN, D_MODEL), F32),
            pltpu.VMEM((TM_FFN, D_MODEL), F32),
            pltpu.VMEM((2 * D_FF // LANES, SHORT_HALO + TM_FFN, LANES), F32),
        ],
        compiler_params=_compiler_params(),
        name=f"ffn_l{layer}",
    )(x, p, *operands, gfin)


def kernel(x, p, g_mix, w_in, gmlp_ln_g, gmlp_ln_b, gmlp_w_s, gmlp_b_s, conf_w_dw, conf_b_dw, conf_ln_g, conf_ln_b, short_w, pool_w, pool_scale, w_branch, w_out, g_ffn, ffn_w_up, ffn_w_conv, ffn_b_conv, ffn_w_down, g_ple, ple_w_gate, ple_w_proj, g_final):
    depth = w_in.shape[0]
    assert x.shape[1] % TM_MIX == 0 and TM_MIX % CHUNK == 0 and x.shape[1] % TM_FFN == 0
    row = lambda a: a.reshape(depth, 1, -1)

    pool_bd = jnp.zeros((depth, BW, BW), F32)
    for g in range(N_GROUPS):
        sl = slice(g * GROUP_DIM, (g + 1) * GROUP_DIM)
        pool_bd = pool_bd.at[:, sl, sl].set(pool_w[:, g])
    mixer_params = [
        row(g_mix),
        _Streamed(w_in),
        row(gmlp_ln_g), row(gmlp_ln_b),
        gmlp_w_s.transpose(0, 2, 1, 3).reshape(depth, CHUNK, N_GROUPS * CHUNK).astype(BF16),
        jnp.repeat(gmlp_b_s.transpose(0, 2, 1), GROUP_DIM, axis=2),
        conf_w_dw, row(conf_b_dw), row(conf_ln_g), row(conf_ln_b),
        short_w,
        pool_bd.astype(BF16), row(pool_scale),
        _Streamed(w_branch.reshape(depth, N_BRANCH * BW, D_MODEL)), _Streamed(w_out),
    ]
    ffn_params = [
        row(g_ffn), _Streamed(ffn_w_up), ffn_w_conv, row(ffn_b_conv),
        _Streamed(ffn_w_down), row(g_ple), _Streamed(ple_w_gate), _Streamed(ple_w_proj),
    ]
    gfin = g_final.reshape(1, -1)

    for i in range(depth):
        x = _mixer_call(x, i, mixer_params)
        x = _ffn_call(x, p, i, ffn_params, gfin, final_norm=(i == depth - 1))
    return x
```

```python
import functools
from typing import NamedTuple

import jax
import jax.numpy as jnp
from jax import lax
from jax.experimental import pallas as pl
from jax.experimental.pallas import tpu as pltpu

D_MODEL = 1024
PLE_DIM = 256
N_BRANCH = 4
BW = 256
N_GROUPS = 4
GROUP_DIM = BW // N_GROUPS
GROUP_SHIFT = GROUP_DIM.bit_length() - 1
CHUNK = 128
CONF_K = 31
SHORT_K = 3
FFN_K = 3
D_FF = 2816
POOL_WINDOWS = (2, 4, 8, 16)
EPS = 1e-6

A_OFF, B_OFF, C_OFF, D_OFF, G_OFF = 0, 2 * BW, 4 * BW, 7 * BW, 8 * BW
IN_COLS = G_OFF + N_BRANCH * D_MODEL

SUBLANES = 8
LANES = 128
MIX_TILES = BW // LANES
TM_MIX = 512
TM_FFN = 512
GATE_DTYPE = jnp.float32
CONF_HALO = 32
SHORT_HALO = 8
POOL_HALO = 24
assert POOL_HALO - SUBLANES >= max(POOL_WINDOWS) - 2
assert LANES == 2 * GROUP_DIM and len(POOL_WINDOWS) == 2 * MIX_TILES
FFN_COLS = 256
DOWN_GROUP = 2
DOWN_LAG = 6
PLE_ROW_BLOCKS = 4
STREAM_ROWS = 256
WIDE_STREAM_ROWS = 128
VMEM_LIMIT_BYTES = 56 * 1024 * 1024

BF16 = jnp.bfloat16
F32 = jnp.float32


def _dot(a, b):
    return jnp.dot(a, b, preferred_element_type=F32)


def _rms_norm(x, g):
    ms = jnp.mean(x * x, axis=-1, keepdims=True)
    return x * lax.rsqrt(ms + EPS) * g


def _stream_to_bf16(src, dst, rows):
    n_chunks = src.shape[0] // rows
    assert n_chunks * rows == src.shape[0]

    def scoped(stage, sem):
        def copy(i, slot):
            return pltpu.make_async_copy(src.at[pl.ds(i * rows, rows), :], stage.at[slot],
                                         sem.at[slot])

        copy(0, 0).start()

        def body(i, carry):
            slot = lax.rem(i, 2)

            @pl.when(i + 1 < n_chunks)
            def _():
                copy(i + 1, 1 - slot).start()

            copy(i, slot).wait()
            dst[pl.ds(pl.multiple_of(i * rows, rows), rows), :] = stage[slot].astype(BF16)
            return carry

        lax.fori_loop(0, n_chunks, body, 0)

    pl.run_scoped(scoped, pltpu.VMEM((2, rows, src.shape[1]), F32),
                  pltpu.SemaphoreType.DMA((2,)))


def _first_step():
    return (pl.program_id(0) == 0) & (pl.program_id(1) == 0)


def _mixer_kernel(x_ref, g_ref, win_hbm, lng_ref, lnb_ref, ws_ref, bs_ref,
                  cw_ref, cb_ref, clg_ref, clb_ref, sw_ref, pw_ref, ps_ref,
                  wbr_hbm, wout_hbm, o_ref,
                  win_ref, wbr_ref, wout_ref,
                  h_buf, y_buf, gate_buf, conf_buf, short_buf, pool_buf, *, layer):
    s = pl.program_id(1)
    tm = x_ref.shape[0]

    @pl.when(_first_step())
    def _():
        _stream_to_bf16(win_hbm.at[layer], win_ref, WIDE_STREAM_ROWS)
        _stream_to_bf16(wbr_hbm.at[layer], wbr_ref, STREAM_ROWS)
        _stream_to_bf16(wout_hbm.at[layer], wout_ref, STREAM_ROWS)

    @pl.when(s == 0)
    def _():
        conf_buf[:, 0:CONF_HALO, :] = jnp.zeros((MIX_TILES, CONF_HALO, LANES), F32)
        short_buf[:, 0:SHORT_HALO, :] = jnp.zeros((MIX_TILES, SHORT_HALO, LANES), F32)
        pool_buf[:, 0:POOL_HALO, :] = jnp.zeros((MIX_TILES, POOL_HALO, LANES), F32)

    def store_tiles(buf, halo, value):
        for c in range(MIX_TILES):
            buf[c, halo:halo + tm, :] = value[:, c * LANES:(c + 1) * LANES]

    h_buf[...] = _rms_norm(x_ref[...], g_ref[...]).astype(BF16)

    z = _dot(h_buf[...], win_ref[:, 0:G_OFF])
    zb = z[:, B_OFF:B_OFF + 2 * BW]
    store_tiles(conf_buf, CONF_HALO, zb[:, :BW] * jax.nn.sigmoid(zb[:, BW:]))
    zc = z[:, C_OFF:C_OFF + 3 * BW]
    store_tiles(short_buf, SHORT_HALO, zc[:, BW:2 * BW] * zc[:, 2 * BW:])
    zd = z[:, D_OFF:D_OFF + BW]
    store_tiles(pool_buf, POOL_HALO, zd)
    za = jax.nn.gelu(z[:, A_OFF:A_OFF + 2 * BW])

    def merge_gate(k):
        zg = _dot(h_buf[...], win_ref[:, G_OFF + k * D_MODEL:G_OFF + (k + 1) * D_MODEL])
        gate_buf[k] = jax.nn.sigmoid(zg).astype(gate_buf.dtype)

    merge_gate(0)
    merge_gate(1)

    def causal_taps(buf, halo, w_ref):
        n_taps = w_ref.shape[0]
        base = halo - (n_taps - 1)
        outs = []
        for c in range(MIX_TILES):
            lanes = slice(c * LANES, (c + 1) * LANES)
            acc = buf[c, base:base + tm, :] * w_ref[0:1, lanes]
            for k in range(1, n_taps):
                acc = acc + buf[c, base + k:base + k + tm, :] * w_ref[k:k + 1, lanes]
            buf[c, 0:halo, :] = buf[c, tm:tm + halo, :]
            outs.append(acc)
        return jnp.concatenate(outs, axis=1)

    yb = causal_taps(conf_buf, CONF_HALO, cw_ref) + cb_ref[...]

    y_buf[2] = (zc[:, :BW] * causal_taps(short_buf, SHORT_HALO, sw_ref)).astype(BF16)

    t_pos = s * tm + lax.broadcasted_iota(jnp.int32, (tm, LANES), 0) + 1
    low = lax.broadcasted_iota(jnp.int32, (tm, LANES), 1) < GROUP_DIM
    span = POOL_HALO - SUBLANES + tm
    pooled = []
    for c in range(MIX_TILES):
        w_lo, w_hi = POOL_WINDOWS[2 * c], POOL_WINDOWS[2 * c + 1]
        carry_rows = pool_buf[c, tm:tm + POOL_HALO, :]
        w = 1
        while w < w_hi:
            doubled = (pool_buf[c, SUBLANES:SUBLANES + span, :]
                       + pool_buf[c, SUBLANES - w:SUBLANES - w + span, :])
            pool_buf[c, SUBLANES:SUBLANES + span, :] = doubled
            w *= 2
            if w == w_lo:
                lo_sum = doubled[POOL_HALO - SUBLANES:, :]
        win_sum = jnp.where(low, lo_sum, doubled[POOL_HALO - SUBLANES:, :])
        cnt = jnp.minimum(t_pos, jnp.where(low, w_lo, w_hi)).astype(F32)
        pooled.append(win_sum / cnt - zd[:, c * LANES:(c + 1) * LANES])
        pool_buf[c, 0:POOL_HALO, :] = carry_rows
    pooled = jnp.concatenate(pooled, axis=1).astype(BF16)

    y_buf[3] = (_dot(pooled, pw_ref[...]) * ps_ref[...]).astype(BF16)
    merge_gate(2)
    gi = lax.broadcasted_iota(jnp.int32, (BW, BW), 0) >> GROUP_SHIFT
    gj = lax.broadcasted_iota(jnp.int32, (BW, BW), 1) >> GROUP_SHIFT
    g_avg = jnp.where(gi == gj, 1.0 / GROUP_DIM, 0.0).astype(BF16)
    ybc = yb - _dot(yb.astype(BF16), g_avg)
    merge_gate(3)

    u = za[:, :BW]
    v = za[:, BW:]
    mu = jnp.mean(v, axis=-1, keepdims=True)
    vc = v - mu
    var = jnp.mean(vc * vc, axis=-1, keepdims=True)
    v = (vc * lax.rsqrt(var + EPS) * lng_ref[...] + lnb_ref[...]).astype(BF16)
    row = lax.broadcasted_iota(jnp.int32, (CHUNK, N_GROUPS * CHUNK), 0)
    col = lax.broadcasted_iota(jnp.int32, (CHUNK, N_GROUPS * CHUNK), 1)
    w_cat = jnp.where((col & (CHUNK - 1)) <= row, ws_ref[...], jnp.zeros((), BF16))
    lane_group = lax.broadcasted_iota(jnp.int32, (CHUNK, BW), 1) >> GROUP_SHIFT
    for n in range(tm // CHUNK):
        vn = v[n * CHUNK:(n + 1) * CHUNK, :]
        v_bd = jnp.concatenate(
            [jnp.where(lane_group == g, vn, jnp.zeros((), BF16)) for g in range(N_GROUPS)],
            axis=0)
        mixed = _dot(w_cat, v_bd) + bs_ref[...]
        y_buf[0, n * CHUNK:(n + 1) * CHUNK, :] = (
            u[n * CHUNK:(n + 1) * CHUNK, :] * mixed).astype(BF16)
    gvar = _dot((ybc * ybc).astype(BF16), g_avg)
    ybn = ybc * lax.rsqrt(gvar + EPS) * clg_ref[...] + clb_ref[...]
    y_buf[1] = (ybn * jax.nn.sigmoid(ybn)).astype(BF16)

    merged = None
    for k in (2, 3, 0, 1):
        term = _dot(y_buf[k], wbr_ref[k * BW:(k + 1) * BW, :]) * gate_buf[k].astype(F32)
        merged = term if merged is None else merged + term
    o_ref[...] = x_ref[...] + _dot(merged.astype(BF16), wout_ref[...])


def _ffn_kernel(x_ref, p_ref, gffn_ref, wup_hbm, wconv_ref, bconv_ref, wdown_hbm,
                gple_ref, wpg_hbm, wpp_hbm, gfin_ref, o_ref,
                wup_ref, wdown_ref, wpg_ref, wpp_ref,
                h_buf, act_buf, acc_buf, up_buf, *, layer, final_norm):
    s = pl.program_id(1)
    tm = x_ref.shape[0]

    @pl.when(_first_step())
    def _():
        _stream_to_bf16(wup_hbm.at[layer], wup_ref, WIDE_STREAM_ROWS)
        _stream_to_bf16(wdown_hbm.at[layer], wdown_ref, STREAM_ROWS)
        _stream_to_bf16(wpg_hbm.at[layer], wpg_ref, STREAM_ROWS)
        _stream_to_bf16(wpp_hbm.at[layer], wpp_ref, STREAM_ROWS)

    @pl.when(s == 0)
    def _():
        up_buf[:, 0:SHORT_HALO, :] = jnp.zeros((2 * D_FF // LANES, SHORT_HALO, LANES), F32)

    x = x_ref[...]
    h_buf[...] = _rms_norm(x, gffn_ref[...]).astype(BF16)

    base = SHORT_HALO - (FFN_K - 1)

    def up_chunk(j):
        parts = []
        for part in range(2):
            c0 = part * D_FF + j * FFN_COLS
            up = _dot(h_buf[...], wup_ref[:, c0:c0 + FFN_COLS])
            convs = []
            for c in range(FFN_COLS // LANES):
                tile = c0 // LANES + c
                lanes = slice(tile * LANES, (tile + 1) * LANES)
                up_buf[tile, SHORT_HALO:SHORT_HALO + tm, :] = up[:, c * LANES:(c + 1) * LANES]
                conv = up_buf[tile, base:base + tm, :] * wconv_ref[0:1, lanes]
                for k in range(1, FFN_K):
                    conv = conv + up_buf[tile, base + k:base + k + tm, :] * wconv_ref[k:k + 1, lanes]
                up_buf[tile, 0:SHORT_HALO, :] = up_buf[tile, tm:tm + SHORT_HALO, :]
                convs.append(conv)
            parts.append(jnp.concatenate(convs, axis=1) + bconv_ref[:, c0:c0 + FFN_COLS])
        gate, val = parts
        act_buf[:, j * FFN_COLS:(j + 1) * FFN_COLS] = (
            gate * jax.nn.sigmoid(gate) * val).astype(BF16)

    def down_group(first, last):
        rows = slice(first * FFN_COLS, last * FFN_COLS)
        partial = _dot(act_buf[:, rows], wdown_ref[rows, :])
        acc_buf[...] = (x if first == 0 else acc_buf[...]) + partial

    n_chunks = D_FF // FFN_COLS
    groups = [(a, min(a + DOWN_GROUP, n_chunks)) for a in range(0, n_chunks, DOWN_GROUP)]
    for gi, (first, last) in enumerate(groups):
        for j in range(first, last):
            up_chunk(j)
        if gi >= DOWN_LAG:
            down_group(*groups[gi - DOWN_LAG])
    for g in groups[len(groups) - DOWN_LAG:]:
        down_group(*g)

    proj = _dot(p_ref[...].astype(BF16), wpp_ref[...])
    block = tm // PLE_ROW_BLOCKS
    for r in range(PLE_ROW_BLOCKS):
        rows = slice(r * block, (r + 1) * block)
        y = acc_buf[rows, :]
        h2 = _rms_norm(y, gple_ref[...]).astype(BF16)
        gate = jax.nn.sigmoid(_dot(h2, wpg_ref[...]))
        y = y + gate * proj[rows, :]
        if final_norm:
            y = _rms_norm(y, gfin_ref[...])
        o_ref[rows, :] = y


def _const_spec(shape, layer):
    nd = len(shape)
    return pl.BlockSpec((None,) + tuple(shape), lambda b, s: (layer,) + (0,) * nd,
                        pipeline_mode=pl.Buffered(1))


def _tile_spec(tm, width):
    return pl.BlockSpec((None, tm, width), lambda b, s: (b, s, 0))


def _compiler_params():
    return pltpu.CompilerParams(dimension_semantics=("arbitrary", "arbitrary"),
                                vmem_limit_bytes=VMEM_LIMIT_BYTES)


class _Streamed(NamedTuple):
    array: jax.Array


def _param_specs(params, layer):
    specs, weight_bufs, operands = [], [], []
    for q in params:
        if isinstance(q, _Streamed):
            specs.append(pl.BlockSpec(memory_space=pl.ANY))
            weight_bufs.append(pltpu.VMEM(q.array.shape[1:], BF16))
            operands.append(q.array)
        else:
            specs.append(_const_spec(q.shape[1:], layer))
            operands.append(q)
    return specs, weight_bufs, operands


def _mixer_call(x, layer, params):
    batch, seq, _ = x.shape
    param_specs, weight_bufs, operands = _param_specs(params, layer)
    return pl.pallas_call(
        functools.partial(_mixer_kernel, layer=layer),
        grid=(batch, seq // TM_MIX),
        in_specs=[_tile_spec(TM_MIX, D_MODEL)] + param_specs,
        out_specs=_tile_spec(TM_MIX, D_MODEL),
        out_shape=jax.ShapeDtypeStruct(x.shape, F32),
        scratch_shapes=weight_bufs + [
            pltpu.VMEM((TM_MIX, D_MODEL), BF16),
            pltpu.VMEM((N_BRANCH, TM_MIX, BW), BF16),
            pltpu.VMEM((N_BRANCH, TM_MIX, D_MODEL), GATE_DTYPE),
            pltpu.VMEM((MIX_TILES, CONF_HALO + TM_MIX, LANES), F32),
            pltpu.VMEM((MIX_TILES, SHORT_HALO + TM_MIX, LANES), F32),
            pltpu.VMEM((MIX_TILES, POOL_HALO + TM_MIX, LANES), F32),
        ],
        compiler_params=_compiler_params(),
        name=f"mixer_l{layer}",
    )(x, *operands)


def _ffn_call(x, p, layer, params, gfin, final_norm):
    batch, seq, _ = x.shape
    param_specs, weight_bufs, operands = _param_specs(params, layer)
    in_specs = ([_tile_spec(TM_FFN, D_MODEL),
                 pl.BlockSpec((None, None, TM_FFN, PLE_DIM), lambda b, s: (layer, b, s, 0))]
                + param_specs
                + [pl.BlockSpec(gfin.shape, lambda b, s: (0, 0), pipeline_mode=pl.Buffered(1))])
    return pl.pallas_call(
        functools.partial(_ffn_kernel, layer=layer, final_norm=final_norm),
        grid=(batch, seq // TM_FFN),
        in_specs=in_specs,
        out_specs=_tile_spec(TM_FFN, D_MODEL),
        out_shape=jax.ShapeDtypeStruct(x.shape, F32),
        scratch_shapes=weight_bufs + [
            pltpu.VMEM((TM_FFN, D_MODEL), BF16),
            pltpu.VMEM((TM_FFN, D_FF), BF16),
            pltpu.VMEM((TM_FFN, D_MODEL), F32),
            pltpu.VMEM((2 * D_FF // LANES, SHORT_HALO + TM_FFN, LANES), F32),
        ],
        compiler_params=_compiler_params(),
        name=f"ffn_l{layer}",
    )(x, p, *operands, gfin)


def kernel(x, p, g_mix, w_in, gmlp_ln_g, gmlp_ln_b, gmlp_w_s, gmlp_b_s, conf_w_dw, conf_b_dw, conf_ln_g, conf_ln_b, short_w, pool_w, pool_scale, w_branch, w_out, g_ffn, ffn_w_up, ffn_w_conv, ffn_b_conv, ffn_w_down, g_ple, ple_w_gate, ple_w_proj, g_final):
    depth = w_in.shape[0]
    assert x.shape[1] % TM_MIX == 0 and TM_MIX % CHUNK == 0 and x.shape[1] % TM_FFN == 0
    row = lambda a: a.reshape(depth, 1, -1)

    pool_bd = jnp.zeros((depth, BW, BW), F32)
    for g in range(N_GROUPS):
        sl = slice(g * GROUP_DIM, (g + 1) * GROUP_DIM)
        pool_bd = pool_bd.at[:, sl, sl].set(pool_w[:, g])
    mixer_params = [
        row(g_mix),
        _Streamed(w_in),
        row(gmlp_ln_g), row(gmlp_ln_b),
        gmlp_w_s.transpose(0, 2, 1, 3).reshape(depth, CHUNK, N_GROUPS * CHUNK).astype(BF16),
        jnp.repeat(gmlp_b_s.transpose(0, 2, 1), GROUP_DIM, axis=2),
        conf_w_dw, row(conf_b_dw), row(conf_ln_g), row(conf_ln_b),
        short_w,
        pool_bd.astype(BF16), row(pool_scale),
        _Streamed(w_branch.reshape(depth, N_BRANCH * BW, D_MODEL)), _Streamed(w_out),
    ]
    ffn_params = [
        row(g_ffn), _Streamed(ffn_w_up), ffn_w_conv, row(ffn_b_conv),
        _Streamed(ffn_w_down), row(g_ple), _Streamed(ple_w_gate), _Streamed(ple_w_proj),
    ]
    gfin = g_final.reshape(1, -1)

    for i in range(depth):
        x = _mixer_call(x, i, mixer_params)
        x = _ffn_call(x, p, i, ffn_params, gfin, final_norm=(i == depth - 1))
    return x
```

```python
import functools
from typing import NamedTuple

import jax
import jax.numpy as jnp
from jax import lax
from jax.experimental import pallas as pl
from jax.experimental.pallas import tpu as pltpu

D_MODEL = 1024
PLE_DIM = 256
N_BRANCH = 4
BW = 256
N_GROUPS = 4
GROUP_DIM = BW // N_GROUPS
GROUP_SHIFT = GROUP_DIM.bit_length() - 1
CHUNK = 128
CONF_K = 31
SHORT_K = 3
FFN_K = 3
D_FF = 2816
POOL_WINDOWS = (2, 4, 8, 16)
EPS = 1e-6

A_OFF, B_OFF, C_OFF, D_OFF, G_OFF = 0, 2 * BW, 4 * BW, 7 * BW, 8 * BW
IN_COLS = G_OFF + N_BRANCH * D_MODEL

SUBLANES = 8
LANES = 128
MIX_TILES = BW // LANES
TM_MIX = 512
TM_FFN = 512
GATE_DTYPE = jnp.float32
CONF_HALO = 32
SHORT_HALO = 8
POOL_HALO = 24
assert POOL_HALO - SUBLANES >= max(POOL_WINDOWS) - 2
assert LANES == 2 * GROUP_DIM and len(POOL_WINDOWS) == 2 * MIX_TILES
FFN_COLS = 256
DOWN_GROUP = 11
DOWN_LAG = 6
PLE_ROW_BLOCKS = 4
STREAM_ROWS = 256
WIDE_STREAM_ROWS = 128
VMEM_LIMIT_BYTES = 56 * 1024 * 1024

BF16 = jnp.bfloat16
F32 = jnp.float32


def _dot(a, b):
    return jnp.dot(a, b, preferred_element_type=F32)


def _rms_norm(x, g):
    ms = jnp.mean(x * x, axis=-1, keepdims=True)
    return x * lax.rsqrt(ms + EPS) * g


def _stream_to_bf16(src, dst, rows):
    n_chunks = src.shape[0] // rows
    assert n_chunks * rows == src.shape[0]

    def scoped(stage, sem):
        def copy(i, slot):
            return pltpu.make_async_copy(src.at[pl.ds(i * rows, rows), :], stage.at[slot],
                                         sem.at[slot])

        copy(0, 0).start()

        def body(i, carry):
            slot = lax.rem(i, 2)

            @pl.when(i + 1 < n_chunks)
            def _():
                copy(i + 1, 1 - slot).start()

            copy(i, slot).wait()
            dst[pl.ds(pl.multiple_of(i * rows, rows), rows), :] = stage[slot].astype(BF16)
            return carry

        lax.fori_loop(0, n_chunks, body, 0)

    pl.run_scoped(scoped, pltpu.VMEM((2, rows, src.shape[1]), F32),
                  pltpu.SemaphoreType.DMA((2,)))


def _first_step():
    return (pl.program_id(0) == 0) & (pl.program_id(1) == 0)


def _mixer_kernel(x_ref, g_ref, win_hbm, lng_ref, lnb_ref, ws_ref, bs_ref,
                  cw_ref, cb_ref, clg_ref, clb_ref, sw_ref, pw_ref, ps_ref,
                  wbr_hbm, wout_hbm, o_ref,
                  win_ref, wbr_ref, wout_ref,
                  h_buf, y_buf, gate_buf, conf_buf, short_buf, pool_buf, *, layer):
    s = pl.program_id(1)
    tm = x_ref.shape[0]

    @pl.when(_first_step())
    def _():
        _stream_to_bf16(win_hbm.at[layer], win_ref, WIDE_STREAM_ROWS)
        _stream_to_bf16(wbr_hbm.at[layer], wbr_ref, STREAM_ROWS)
        _stream_to_bf16(wout_hbm.at[layer], wout_ref, STREAM_ROWS)

    @pl.when(s == 0)
    def _():
        conf_buf[:, 0:CONF_HALO, :] = jnp.zeros((MIX_TILES, CONF_HALO, LANES), F32)
        short_buf[:, 0:SHORT_HALO, :] = jnp.zeros((MIX_TILES, SHORT_HALO, LANES), F32)
        pool_buf[:, 0:POOL_HALO, :] = jnp.zeros((MIX_TILES, POOL_HALO, LANES), F32)

    def store_tiles(buf, halo, value):
        for c in range(MIX_TILES):
            buf[c, halo:halo + tm, :] = value[:, c * LANES:(c + 1) * LANES]

    h_buf[...] = _rms_norm(x_ref[...], g_ref[...]).astype(BF16)

    z = _dot(h_buf[...], win_ref[:, 0:G_OFF])
    zb = z[:, B_OFF:B_OFF + 2 * BW]
    store_tiles(conf_buf, CONF_HALO, zb[:, :BW] * jax.nn.sigmoid(zb[:, BW:]))
    zc = z[:, C_OFF:C_OFF + 3 * BW]
    store_tiles(short_buf, SHORT_HALO, zc[:, BW:2 * BW] * zc[:, 2 * BW:])
    zd = z[:, D_OFF:D_OFF + BW]
    store_tiles(pool_buf, POOL_HALO, zd)
    za = jax.nn.gelu(z[:, A_OFF:A_OFF + 2 * BW])

    def merge_gate(k):
        zg = _dot(h_buf[...], win_ref[:, G_OFF + k * D_MODEL:G_OFF + (k + 1) * D_MODEL])
        gate_buf[k] = jax.nn.sigmoid(zg).astype(gate_buf.dtype)

    merge_gate(0)
    merge_gate(1)

    def causal_taps(buf, halo, w_ref):
        n_taps = w_ref.shape[0]
        base = halo - (n_taps - 1)
        outs = []
        for c in range(MIX_TILES):
            lanes = slice(c * LANES, (c + 1) * LANES)
            acc = buf[c, base:base + tm, :] * w_ref[0:1, lanes]
            for k in range(1, n_taps):
                acc = acc + buf[c, base + k:base + k + tm, :] * w_ref[k:k + 1, lanes]
            buf[c, 0:halo, :] = buf[c, tm:tm + halo, :]
            outs.append(acc)
        return jnp.concatenate(outs, axis=1)

    yb = causal_taps(conf_buf, CONF_HALO, cw_ref) + cb_ref[...]

    y_buf[2] = (zc[:, :BW] * causal_taps(short_buf, SHORT_HALO, sw_ref)).astype(BF16)

    t_pos = s * tm + lax.broadcasted_iota(jnp.int32, (tm, LANES), 0) + 1
    low = lax.broadcasted_iota(jnp.int32, (tm, LANES), 1) < GROUP_DIM
    span = POOL_HALO - SUBLANES + tm
    pooled = []
    for c in range(MIX_TILES):
        w_lo, w_hi = POOL_WINDOWS[2 * c], POOL_WINDOWS[2 * c + 1]
        carry_rows = pool_buf[c, tm:tm + POOL_HALO, :]
        w = 1
        while w < w_hi:
            doubled = (pool_buf[c, SUBLANES:SUBLANES + span, :]
                       + pool_buf[c, SUBLANES - w:SUBLANES - w + span, :])
            pool_buf[c, SUBLANES:SUBLANES + span, :] = doubled
            w *= 2
            if w == w_lo:
                lo_sum = doubled[POOL_HALO - SUBLANES:, :]
        win_sum = jnp.where(low, lo_sum, doubled[POOL_HALO - SUBLANES:, :])
        cnt = jnp.minimum(t_pos, jnp.where(low, w_lo, w_hi)).astype(F32)
        pooled.append(win_sum / cnt - zd[:, c * LANES:(c + 1) * LANES])
        pool_buf[c, 0:POOL_HALO, :] = carry_rows
    pooled = jnp.concatenate(pooled, axis=1).astype(BF16)

    y_buf[3] = (_dot(pooled, pw_ref[...]) * ps_ref[...]).astype(BF16)
    merge_gate(2)
    gi = lax.broadcasted_iota(jnp.int32, (BW, BW), 0) >> GROUP_SHIFT
    gj = lax.broadcasted_iota(jnp.int32, (BW, BW), 1) >> GROUP_SHIFT
    g_avg = jnp.where(gi == gj, 1.0 / GROUP_DIM, 0.0).astype(BF16)
    ybc = yb - _dot(yb.astype(BF16), g_avg)
    merge_gate(3)

    u = za[:, :BW]
    v = za[:, BW:]
    mu = jnp.mean(v, axis=-1, keepdims=True)
    vc = v - mu
    var = jnp.mean(vc * vc, axis=-1, keepdims=True)
    v = (vc * lax.rsqrt(var + EPS) * lng_ref[...] + lnb_ref[...]).astype(BF16)
    row = lax.broadcasted_iota(jnp.int32, (CHUNK, N_GROUPS * CHUNK), 0)
    col = lax.broadcasted_iota(jnp.int32, (CHUNK, N_GROUPS * CHUNK), 1)
    w_cat = jnp.where((col & (CHUNK - 1)) <= row, ws_ref[...], jnp.zeros((), BF16))
    lane_group = lax.broadcasted_iota(jnp.int32, (CHUNK, BW), 1) >> GROUP_SHIFT
    for n in range(tm // CHUNK):
        vn = v[n * CHUNK:(n + 1) * CHUNK, :]
        v_bd = jnp.concatenate(
            [jnp.where(lane_group == g, vn, jnp.zeros((), BF16)) for g in range(N_GROUPS)],
            axis=0)
        mixed = _dot(w_cat, v_bd) + bs_ref[...]
        y_buf[0, n * CHUNK:(n + 1) * CHUNK, :] = (
            u[n * CHUNK:(n + 1) * CHUNK, :] * mixed).astype(BF16)
    gvar = _dot((ybc * ybc).astype(BF16), g_avg)
    ybn = ybc * lax.rsqrt(gvar + EPS) * clg_ref[...] + clb_ref[...]
    y_buf[1] = (ybn * jax.nn.sigmoid(ybn)).astype(BF16)

    merged = None
    for k in (2, 3, 0, 1):
        term = _dot(y_buf[k], wbr_ref[k * BW:(k + 1) * BW, :]) * gate_buf[k].astype(F32)
        merged = term if merged is None else merged + term
    o_ref[...] = x_ref[...] + _dot(merged.astype(BF16), wout_ref[...])


def _ffn_kernel(x_ref, p_ref, gffn_ref, wup_hbm, wconv_ref, bconv_ref, wdown_hbm,
                gple_ref, wpg_hbm, wpp_hbm, gfin_ref, o_ref,
                wup_ref, wdown_ref, wpg_ref, wpp_ref,
                h_buf, act_buf, acc_buf, up_buf, *, layer, final_norm):
    s = pl.program_id(1)
    tm = x_ref.shape[0]

    @pl.when(_first_step())
    def _():
        _stream_to_bf16(wup_hbm.at[layer], wup_ref, WIDE_STREAM_ROWS)
        _stream_to_bf16(wdown_hbm.at[layer], wdown_ref, STREAM_ROWS)
        _stream_to_bf16(wpg_hbm.at[layer], wpg_ref, STREAM_ROWS)
        _stream_to_bf16(wpp_hbm.at[layer], wpp_ref, STREAM_ROWS)

    @pl.when(s == 0)
    def _():
        up_buf[:, 0:SHORT_HALO, :] = jnp.zeros((2 * D_FF // LANES, SHORT_HALO, LANES), F32)

    x = x_ref[...]
    h_buf[...] = _rms_norm(x, gffn_ref[...]).astype(BF16)

    base = SHORT_HALO - (FFN_K - 1)

    def up_chunk(j):
        parts = []
        for part in range(2):
            c0 = part * D_FF + j * FFN_COLS
            up = _dot(h_buf[...], wup_ref[:, c0:c0 + FFN_COLS])
            convs = []
            for c in range(FFN_COLS // LANES):
                tile = c0 // LANES + c
                lanes = slice(tile * LANES, (tile + 1) * LANES)
                up_buf[tile, SHORT_HALO:SHORT_HALO + tm, :] = up[:, c * LANES:(c + 1) * LANES]
                conv = up_buf[tile, base:base + tm, :] * wconv_ref[0:1, lanes]
                for k in range(1, FFN_K):
                    conv = conv + up_buf[tile, base + k:base + k + tm, :] * wconv_ref[k:k + 1, lanes]
                up_buf[tile, 0:SHORT_HALO, :] = up_buf[tile, tm:tm + SHORT_HALO, :]
                convs.append(conv)
            parts.append(jnp.concatenate(convs, axis=1) + bconv_ref[:, c0:c0 + FFN_COLS])
        gate, val = parts
        act_buf[:, j * FFN_COLS:(j + 1) * FFN_COLS] = (
            gate * jax.nn.sigmoid(gate) * val).astype(BF16)

    def down_group(first, last):
        rows = slice(first * FFN_COLS, last * FFN_COLS)
        partial = _dot(act_buf[:, rows], wdown_ref[rows, :])
        acc_buf[...] = (x if first == 0 else acc_buf[...]) + partial

    n_chunks = D_FF // FFN_COLS
    groups = [(a, min(a + DOWN_GROUP, n_chunks)) for a in range(0, n_chunks, DOWN_GROUP)]
    for gi, (first, last) in enumerate(groups):
        for j in range(first, last):
            up_chunk(j)
        if gi >= DOWN_LAG:
            down_group(*groups[gi - DOWN_LAG])
    for g in groups[len(groups) - DOWN_LAG:]:
        down_group(*g)

    proj = _dot(p_ref[...].astype(BF16), wpp_ref[...])
    block = tm // PLE_ROW_BLOCKS
    for r in range(PLE_ROW_BLOCKS):
        rows = slice(r * block, (r + 1) * block)
        y = acc_buf[rows, :]
        h2 = _rms_norm(y, gple_ref[...]).astype(BF16)
        gate = jax.nn.sigmoid(_dot(h2, wpg_ref[...]))
        y = y + gate * proj[rows, :]
        if final_norm:
            y = _rms_norm(y, gfin_ref[...])
        o_ref[rows, :] = y


def _const_spec(shape, layer):
    nd = len(shape)
    return pl.BlockSpec((None,) + tuple(shape), lambda b, s: (layer,) + (0,) * nd,
                        pipeline_mode=pl.Buffered(1))


def _tile_spec(tm, width):
    return pl.BlockSpec((None, tm, width), lambda b, s: (b, s, 0))


def _compiler_params():
    return pltpu.CompilerParams(dimension_semantics=("arbitrary", "arbitrary"),
                                vmem_limit_bytes=VMEM_LIMIT_BYTES)


class _Streamed(NamedTuple):
    array: jax.Array


def _param_specs(params, layer):
    specs, weight_bufs, operands = [], [], []
    for q in params:
        if isinstance(q, _Streamed):
            specs.append(pl.BlockSpec(memory_space=pl.ANY))
            weight_bufs.append(pltpu.VMEM(q.array.shape[1:], BF16))
            operands.append(q.array)
        else:
            specs.append(_const_spec(q.shape[1:], layer))
            operands.append(q)
    return specs, weight_bufs, operands


def _mixer_call(x, layer, params):
    batch, seq, _ = x.shape
    param_specs, weight_bufs, operands = _param_specs(params, layer)
    return pl.pallas_call(
        functools.partial(_mixer_kernel, layer=layer),
        grid=(batch, seq // TM_MIX),
        in_specs=[_tile_spec(TM_MIX, D_MODEL)] + param_specs,
        out_specs=_tile_spec(TM_MIX, D_MODEL),
        out_shape=jax.ShapeDtypeStruct(x.shape, F32),
        scratch_shapes=weight_bufs + [
            pltpu.VMEM((TM_MIX, D_MODEL), BF16),
            pltpu.VMEM((N_BRANCH, TM_MIX, BW), BF16),
            pltpu.VMEM((N_BRANCH, TM_MIX, D_MODEL), GATE_DTYPE),
            pltpu.VMEM((MIX_TILES, CONF_HALO + TM_MIX, LANES), F32),
            pltpu.VMEM((MIX_TILES, SHORT_HALO + TM_MIX, LANES), F32),
            pltpu.VMEM((MIX_TILES, POOL_HALO + TM_MIX, LANES), F32),
        ],
        compiler_params=_compiler_params(),
        name=f"mixer_l{layer}",
    )(x, *operands)


def _ffn_call(x, p, layer, params, gfin, final_norm):
    batch, seq, _ = x.shape
    param_specs, weight_bufs, operands = _param_specs(params, layer)
    in_specs = ([_tile_spec(TM_FFN, D_MODEL),
                 pl.BlockSpec((None, None, TM_FFN, PLE_DIM), lambda b, s: (layer, b, s, 0))]
                + param_specs
                + [pl.BlockSpec(gfin.shape, lambda b, s: (0, 0), pipeline_mode=pl.Buffered(1))])
    return pl.pallas_call(
        functools.partial(_ffn_kernel, layer=layer, final_norm=final_norm),
        grid=(batch, seq // TM_FFN),
        in_specs=in_specs,
        out_specs=_tile_spec(TM_FFN, D_MODEL),
        out_shape=jax.ShapeDtypeStruct(x.shape, F32),
        scratch_shapes=weight_bufs + [
            pltpu.VMEM((TM_FFN, D_MODEL), BF16),
            pltpu.VMEM((TM_FFN, D_FF), BF16),
            pltpu.VMEM((TM_FFN, D_MODEL), F32),
            pltpu.VMEM((2 * D_FF // LANES, SHORT_HALO + TM_FFN, LANES), F32),
        ],
        compiler_params=_compiler_params(),
        name=f"ffn_l{layer}",
    )(x, p, *operands, gfin)


def kernel(x, p, g_mix, w_in, gmlp_ln_g, gmlp_ln_b, gmlp_w_s, gmlp_b_s, conf_w_dw, conf_b_dw, conf_ln_g, conf_ln_b, short_w, pool_w, pool_scale, w_branch, w_out, g_ffn, ffn_w_up, ffn_w_conv, ffn_b_conv, ffn_w_down, g_ple, ple_w_gate, ple_w_proj, g_final):
    depth = w_in.shape[0]
    assert x.shape[1] % TM_MIX == 0 and TM_MIX % CHUNK == 0 and x.shape[1] % TM_FFN == 0
    row = lambda a: a.reshape(depth, 1, -1)

    pool_bd = jnp.zeros((depth, BW, BW), F32)
    for g in range(N_GROUPS):
        sl = slice(g * GROUP_DIM, (g + 1) * GROUP_DIM)
        pool_bd = pool_bd.at[:, sl, sl].set(pool_w[:, g])
    mixer_params = [
        row(g_mix),
        _Streamed(w_in),
        row(gmlp_ln_g), row(gmlp_ln_b),
        gmlp_w_s.transpose(0, 2, 1, 3).reshape(depth, CHUNK, N_GROUPS * CHUNK).astype(BF16),
        jnp.repeat(gmlp_b_s.transpose(0, 2, 1), GROUP_DIM, axis=2),
        conf_w_dw, row(conf_b_dw), row(conf_ln_g), row(conf_ln_b),
        short_w,
        pool_bd.astype(BF16), row(pool_scale),
        _Streamed(w_branch.reshape(depth, N_BRANCH * BW, D_MODEL)), _Streamed(w_out),
    ]
    ffn_params = [
        row(g_ffn), _Streamed(ffn_w_up), ffn_w_conv, row(ffn_b_conv),
        _Streamed(ffn_w_down), row(g_ple), _Streamed(ple_w_gate), _Streamed(ple_w_proj),
    ]
    gfin = g_final.reshape(1, -1)

    for i in range(depth):
        x = _mixer_call(x, i, mixer_params)
        x = _ffn_call(x, p, i, ffn_params, gfin, final_norm=(i == depth - 1))
    return x
```

```python
import functools
from typing import NamedTuple

import jax
import jax.numpy as jnp
from jax import lax
from jax.experimental import pallas as pl
from jax.experimental.pallas import tpu as pltpu

D_MODEL = 1024
PLE_DIM = 256
N_BRANCH = 4
BW = 256
N_GROUPS = 4
GROUP_DIM = BW // N_GROUPS
GROUP_SHIFT = GROUP_DIM.bit_length() - 1
CHUNK = 128
CONF_K = 31
SHORT_K = 3
FFN_K = 3
D_FF = 2816
POOL_WINDOWS = (2, 4, 8, 16)
EPS = 1e-6

A_OFF, B_OFF, C_OFF, D_OFF, G_OFF = 0, 2 * BW, 4 * BW, 7 * BW, 8 * BW
IN_COLS = G_OFF + N_BRANCH * D_MODEL

SUBLANES = 8
LANES = 128
MIX_TILES = BW // LANES
TM_MIX = 512
TM_FFN = 512
GATE_DTYPE = jnp.float32
CONF_HALO = 32
SHORT_HALO = 8
POOL_HALO = 24
assert POOL_HALO - SUBLANES >= max(POOL_WINDOWS) - 2
assert LANES == 2 * GROUP_DIM and len(POOL_WINDOWS) == 2 * MIX_TILES
FFN_COLS = 256
DOWN_GROUP = 2
DOWN_LAG = 6
PLE_ROW_BLOCKS = 4
STREAM_ROWS = 256
WIDE_STREAM_ROWS = 128
VMEM_LIMIT_BYTES = 56 * 1024 * 1024

BF16 = jnp.bfloat16
F32 = jnp.float32


def _dot(a, b):
    return jnp.dot(a, b, preferred_element_type=F32)


def _rms_norm(x, g):
    ms = jnp.mean(x * x, axis=-1, keepdims=True)
    return x * lax.rsqrt(ms + EPS) * g


def _stream_to_bf16(src, dst, rows):
    n_chunks = src.shape[0] // rows
    assert n_chunks * rows == src.shape[0]

    def scoped(stage, sem):
        def copy(i, slot):
            return pltpu.make_async_copy(src.at[pl.ds(i * rows, rows), :], stage.at[slot],
                                         sem.at[slot])

        copy(0, 0).start()

        def body(i, carry):
            slot = lax.rem(i, 2)

            @pl.when(i + 1 < n_chunks)
            def _():
                copy(i + 1, 1 - slot).start()

            copy(i, slot).wait()
            dst[pl.ds(pl.multiple_of(i * rows, rows), rows), :] = stage[slot].astype(BF16)
            return carry

        lax.fori_loop(0, n_chunks, body, 0)

    pl.run_scoped(scoped, pltpu.VMEM((2, rows, src.shape[1]), F32),
                  pltpu.SemaphoreType.DMA((2,)))


def _first_step():
    return (pl.program_id(0) == 0) & (pl.program_id(1) == 0)


def _split_rows(rows_ref, widths):
    views, off = [], 0
    for w in widths:
        views.append(rows_ref.at[:, pl.ds(off, w)])
        off += w
    assert off == rows_ref.shape[1]
    return views


MIXER_ROW_WIDTHS = (D_MODEL,) + (BW,) * 6
FFN_ROW_WIDTHS = (D_MODEL, 2 * D_FF, D_MODEL, D_MODEL)


def _mixer_kernel(x_ref, rows_ref, win_hbm, ws_ref, bs_ref, cw_ref, sw_ref, pw_ref,
                  wbr_hbm, wout_hbm, o_ref,
                  win_ref, wbr_ref, wout_ref,
                  h_buf, y_buf, gate_buf, conf_buf, short_buf, pool_buf, *, layer):
    s = pl.program_id(1)
    tm = x_ref.shape[0]
    g_ref, lng_ref, lnb_ref, cb_ref, clg_ref, clb_ref, ps_ref = _split_rows(
        rows_ref, MIXER_ROW_WIDTHS)

    @pl.when(_first_step())
    def _():
        _stream_to_bf16(win_hbm.at[layer], win_ref, WIDE_STREAM_ROWS)
        _stream_to_bf16(wbr_hbm.at[layer], wbr_ref, STREAM_ROWS)
        _stream_to_bf16(wout_hbm.at[layer], wout_ref, STREAM_ROWS)

    @pl.when(s == 0)
    def _():
        conf_buf[:, 0:CONF_HALO, :] = jnp.zeros((MIX_TILES, CONF_HALO, LANES), F32)
        short_buf[:, 0:SHORT_HALO, :] = jnp.zeros((MIX_TILES, SHORT_HALO, LANES), F32)
        pool_buf[:, 0:POOL_HALO, :] = jnp.zeros((MIX_TILES, POOL_HALO, LANES), F32)

    def store_tiles(buf, halo, value):
        for c in range(MIX_TILES):
            buf[c, halo:halo + tm, :] = value[:, c * LANES:(c + 1) * LANES]

    h_buf[...] = _rms_norm(x_ref[...], g_ref[...]).astype(BF16)

    z = _dot(h_buf[...], win_ref[:, 0:G_OFF])
    zb = z[:, B_OFF:B_OFF + 2 * BW]
    store_tiles(conf_buf, CONF_HALO, zb[:, :BW] * jax.nn.sigmoid(zb[:, BW:]))
    zc = z[:, C_OFF:C_OFF + 3 * BW]
    store_tiles(short_buf, SHORT_HALO, zc[:, BW:2 * BW] * zc[:, 2 * BW:])
    zd = z[:, D_OFF:D_OFF + BW]
    store_tiles(pool_buf, POOL_HALO, zd)
    za = jax.nn.gelu(z[:, A_OFF:A_OFF + 2 * BW])

    def merge_gate(k):
        zg = _dot(h_buf[...], win_ref[:, G_OFF + k * D_MODEL:G_OFF + (k + 1) * D_MODEL])
        gate_buf[k] = jax.nn.sigmoid(zg).astype(gate_buf.dtype)

    merge_gate(0)
    merge_gate(1)

    def causal_taps(buf, halo, w_ref):
        n_taps = w_ref.shape[0]
        base = halo - (n_taps - 1)
        outs = []
        for c in range(MIX_TILES):
            lanes = slice(c * LANES, (c + 1) * LANES)
            acc = buf[c, base:base + tm, :] * w_ref[0:1, lanes]
            for k in range(1, n_taps):
                acc = acc + buf[c, base + k:base + k + tm, :] * w_ref[k:k + 1, lanes]
            buf[c, 0:halo, :] = buf[c, tm:tm + halo, :]
            outs.append(acc)
        return jnp.concatenate(outs, axis=1)

    yb = causal_taps(conf_buf, CONF_HALO, cw_ref) + cb_ref[...]

    y_buf[2] = (zc[:, :BW] * causal_taps(short_buf, SHORT_HALO, sw_ref)).astype(BF16)

    t_pos = s * tm + lax.broadcasted_iota(jnp.int32, (tm, LANES), 0) + 1
    low = lax.broadcasted_iota(jnp.int32, (tm, LANES), 1) < GROUP_DIM
    span = POOL_HALO - SUBLANES + tm
    pooled = []
    for c in range(MIX_TILES):
        w_lo, w_hi = POOL_WINDOWS[2 * c], POOL_WINDOWS[2 * c + 1]
        carry_rows = pool_buf[c, tm:tm + POOL_HALO, :]
        w = 1
        while w < w_hi:
            doubled = (pool_buf[c, SUBLANES:SUBLANES + span, :]
                       + pool_buf[c, SUBLANES - w:SUBLANES - w + span, :])
            pool_buf[c, SUBLANES:SUBLANES + span, :] = doubled
            w *= 2
            if w == w_lo:
                lo_sum = doubled[POOL_HALO - SUBLANES:, :]
        win_sum = jnp.where(low, lo_sum, doubled[POOL_HALO - SUBLANES:, :])
        cnt = jnp.minimum(t_pos, jnp.where(low, w_lo, w_hi)).astype(F32)
        pooled.append(win_sum / cnt - zd[:, c * LANES:(c + 1) * LANES])
        pool_buf[c, 0:POOL_HALO, :] = carry_rows
    pooled = jnp.concatenate(pooled, axis=1).astype(BF16)

    y_buf[3] = (_dot(pooled, pw_ref[...]) * ps_ref[...]).astype(BF16)
    merge_gate(2)
    gi = lax.broadcasted_iota(jnp.int32, (BW, BW), 0) >> GROUP_SHIFT
    gj = lax.broadcasted_iota(jnp.int32, (BW, BW), 1) >> GROUP_SHIFT
    g_avg = jnp.where(gi == gj, 1.0 / GROUP_DIM, 0.0).astype(BF16)
    ybc = yb - _dot(yb.astype(BF16), g_avg)
    merge_gate(3)

    u = za[:, :BW]
    v = za[:, BW:]
    mu = jnp.mean(v, axis=-1, keepdims=True)
    vc = v - mu
    var = jnp.mean(vc * vc, axis=-1, keepdims=True)
    v = (vc * lax.rsqrt(var + EPS) * lng_ref[...] + lnb_ref[...]).astype(BF16)
    row = lax.broadcasted_iota(jnp.int32, (CHUNK, N_GROUPS * CHUNK), 0)
    col = lax.broadcasted_iota(jnp.int32, (CHUNK, N_GROUPS * CHUNK), 1)
    w_all = jnp.concatenate([ws_ref[g] for g in range(N_GROUPS)], axis=1).astype(BF16)
    w_cat = jnp.where((col & (CHUNK - 1)) <= row, w_all, jnp.zeros((), BF16))
    lane_group = lax.broadcasted_iota(jnp.int32, (CHUNK, BW), 1) >> GROUP_SHIFT
    for n in range(tm // CHUNK):
        vn = v[n * CHUNK:(n + 1) * CHUNK, :]
        v_bd = jnp.concatenate(
            [jnp.where(lane_group == g, vn, jnp.zeros((), BF16)) for g in range(N_GROUPS)],
            axis=0)
        mixed = _dot(w_cat, v_bd) + bs_ref[...]
        y_buf[0, n * CHUNK:(n + 1) * CHUNK, :] = (
            u[n * CHUNK:(n + 1) * CHUNK, :] * mixed).astype(BF16)
    gvar = _dot((ybc * ybc).astype(BF16), g_avg)
    ybn = ybc * lax.rsqrt(gvar + EPS) * clg_ref[...] + clb_ref[...]
    y_buf[1] = (ybn * jax.nn.sigmoid(ybn)).astype(BF16)

    merged = None
    for k in (2, 3, 0, 1):
        term = _dot(y_buf[k], wbr_ref[k * BW:(k + 1) * BW, :]) * gate_buf[k].astype(F32)
        merged = term if merged is None else merged + term
    o_ref[...] = x_ref[...] + _dot(merged.astype(BF16), wout_ref[...])


def _ffn_kernel(x_ref, p_ref, rows_ref, wup_hbm, wconv_ref, wdown_hbm, wpg_hbm, wpp_hbm, o_ref,
                wup_ref, wdown_ref, wpg_ref, wpp_ref,
                h_buf, act_buf, acc_buf, up_buf, *, layer, final_norm):
    s = pl.program_id(1)
    tm = x_ref.shape[0]
    gffn_ref, bconv_ref, gple_ref, gfin_ref = _split_rows(rows_ref, FFN_ROW_WIDTHS)

    @pl.when(_first_step())
    def _():
        _stream_to_bf16(wup_hbm.at[layer], wup_ref, WIDE_STREAM_ROWS)
        _stream_to_bf16(wdown_hbm.at[layer], wdown_ref, STREAM_ROWS)
        _stream_to_bf16(wpg_hbm.at[layer], wpg_ref, STREAM_ROWS)
        _stream_to_bf16(wpp_hbm.at[layer], wpp_ref, STREAM_ROWS)

    @pl.when(s == 0)
    def _():
        up_buf[:, 0:SHORT_HALO, :] = jnp.zeros((2 * D_FF // LANES, SHORT_HALO, LANES), F32)

    x = x_ref[...]
    h_buf[...] = _rms_norm(x, gffn_ref[...]).astype(BF16)

    base = SHORT_HALO - (FFN_K - 1)

    def up_chunk(j):
        parts = []
        for part in range(2):
            c0 = part * D_FF + j * FFN_COLS
            up = _dot(h_buf[...], wup_ref[:, c0:c0 + FFN_COLS])
            convs = []
            for c in range(FFN_COLS // LANES):
                tile = c0 // LANES + c
                lanes = slice(tile * LANES, (tile + 1) * LANES)
                up_buf[tile, SHORT_HALO:SHORT_HALO + tm, :] = up[:, c * LANES:(c + 1) * LANES]
                conv = up_buf[tile, base:base + tm, :] * wconv_ref[0:1, lanes]
                for k in range(1, FFN_K):
                    conv = conv + up_buf[tile, base + k:base + k + tm, :] * wconv_ref[k:k + 1, lanes]
                up_buf[tile, 0:SHORT_HALO, :] = up_buf[tile, tm:tm + SHORT_HALO, :]
                convs.append(conv)
            parts.append(jnp.concatenate(convs, axis=1) + bconv_ref[:, c0:c0 + FFN_COLS])
        gate, val = parts
        act_buf[:, j * FFN_COLS:(j + 1) * FFN_COLS] = (
            gate * jax.nn.sigmoid(gate) * val).astype(BF16)

    def down_group(first, last):
        rows = slice(first * FFN_COLS, last * FFN_COLS)
        partial = _dot(act_buf[:, rows], wdown_ref[rows, :])
        acc_buf[...] = (x if first == 0 else acc_buf[...]) + partial

    n_chunks = D_FF // FFN_COLS
    groups = [(a, min(a + DOWN_GROUP, n_chunks)) for a in range(0, n_chunks, DOWN_GROUP)]
    for gi, (first, last) in enumerate(groups):
        for j in range(first, last):
            up_chunk(j)
        if gi >= DOWN_LAG:
            down_group(*groups[gi - DOWN_LAG])
    for g in groups[len(groups) - DOWN_LAG:]:
        down_group(*g)

    proj = _dot(p_ref[...].astype(BF16), wpp_ref[...])
    block = tm // PLE_ROW_BLOCKS
    for r in range(PLE_ROW_BLOCKS):
        rows = slice(r * block, (r + 1) * block)
        y = acc_buf[rows, :]
        h2 = _rms_norm(y, gple_ref[...]).astype(BF16)
        gate = jax.nn.sigmoid(_dot(h2, wpg_ref[...]))
        y = y + gate * proj[rows, :]
        if final_norm:
            y = _rms_norm(y, gfin_ref[...])
        o_ref[rows, :] = y


def _const_spec(shape, layer):
    nd = len(shape)
    return pl.BlockSpec((None,) + tuple(shape), lambda b, s: (layer,) + (0,) * nd,
                        pipeline_mode=pl.Buffered(1))


def _tile_spec(tm, width):
    return pl.BlockSpec((None, tm, width), lambda b, s: (b, s, 0))


def _compiler_params():
    return pltpu.CompilerParams(dimension_semantics=("arbitrary", "arbitrary"),
                                vmem_limit_bytes=VMEM_LIMIT_BYTES)


class _Streamed(NamedTuple):
    array: jax.Array


def _param_specs(params, layer):
    specs, weight_bufs, operands = [], [], []
    for q in params:
        if isinstance(q, _Streamed):
            specs.append(pl.BlockSpec(memory_space=pl.ANY))
            weight_bufs.append(pltpu.VMEM(q.array.shape[1:], BF16))
            operands.append(q.array)
        else:
            specs.append(_const_spec(q.shape[1:], layer))
            operands.append(q)
    return specs, weight_bufs, operands


def _mixer_call(x, layer, params):
    batch, seq, _ = x.shape
    param_specs, weight_bufs, operands = _param_specs(params, layer)
    return pl.pallas_call(
        functools.partial(_mixer_kernel, layer=layer),
        grid=(batch, seq // TM_MIX),
        in_specs=[_tile_spec(TM_MIX, D_MODEL)] + param_specs,
        out_specs=_tile_spec(TM_MIX, D_MODEL),
        out_shape=jax.ShapeDtypeStruct(x.shape, F32),
        scratch_shapes=weight_bufs + [
            pltpu.VMEM((TM_MIX, D_MODEL), BF16),
            pltpu.VMEM((N_BRANCH, TM_MIX, BW), BF16),
            pltpu.VMEM((N_BRANCH, TM_MIX, D_MODEL), GATE_DTYPE),
            pltpu.VMEM((MIX_TILES, CONF_HALO + TM_MIX, LANES), F32),
            pltpu.VMEM((MIX_TILES, SHORT_HALO + TM_MIX, LANES), F32),
            pltpu.VMEM((MIX_TILES, POOL_HALO + TM_MIX, LANES), F32),
        ],
        compiler_params=_compiler_params(),
        name=f"mixer_l{layer}",
    )(x, *operands)


def _ffn_call(x, p, layer, params, final_norm):
    batch, seq, _ = x.shape
    param_specs, weight_bufs, operands = _param_specs(params, layer)
    in_specs = ([_tile_spec(TM_FFN, D_MODEL),
                 pl.BlockSpec((None, None, TM_FFN, PLE_DIM), lambda b, s: (layer, b, s, 0))]
                + param_specs)
    return pl.pallas_call(
        functools.partial(_ffn_kernel, layer=layer, final_norm=final_norm),
        grid=(batch, seq // TM_FFN),
        in_specs=in_specs,
        out_specs=_tile_spec(TM_FFN, D_MODEL),
        out_shape=jax.ShapeDtypeStruct(x.shape, F32),
        scratch_shapes=weight_bufs + [
            pltpu.VMEM((TM_FFN, D_MODEL), BF16),
            pltpu.VMEM((TM_FFN, D_FF), BF16),
            pltpu.VMEM((TM_FFN, D_MODEL), F32),
            pltpu.VMEM((2 * D_FF // LANES, SHORT_HALO + TM_FFN, LANES), F32),
        ],
        compiler_params=_compiler_params(),
        name=f"ffn_l{layer}",
    )(x, p, *operands)


def kernel(x, p, g_mix, w_in, gmlp_ln_g, gmlp_ln_b, gmlp_w_s, gmlp_b_s, conf_w_dw, conf_b_dw, conf_ln_g, conf_ln_b, short_w, pool_w, pool_scale, w_branch, w_out, g_ffn, ffn_w_up, ffn_w_conv, ffn_b_conv, ffn_w_down, g_ple, ple_w_gate, ple_w_proj, g_final):
    depth = w_in.shape[0]
    assert x.shape[1] % TM_MIX == 0 and TM_MIX % CHUNK == 0 and x.shape[1] % TM_FFN == 0
    flat = lambda a: a.reshape(depth, -1)

    mixer_rows = jnp.concatenate(
        [g_mix, gmlp_ln_g, gmlp_ln_b, conf_b_dw, flat(conf_ln_g), flat(conf_ln_b),
         flat(pool_scale)], axis=1)[:, None, :]
    ffn_rows = jnp.concatenate(
        [g_ffn, ffn_b_conv, g_ple, jnp.broadcast_to(g_final, (depth, D_MODEL))],
        axis=1)[:, None, :]
    pool_bd = jnp.einsum('gh,lgcd->lgchd', jnp.eye(N_GROUPS, dtype=F32), pool_w)
    mixer_params = [
        mixer_rows,
        _Streamed(w_in),
        gmlp_w_s,
        jnp.repeat(gmlp_b_s.transpose(0, 2, 1), GROUP_DIM, axis=2),
        conf_w_dw,
        short_w,
        pool_bd.reshape(depth, BW, BW).astype(BF16),
        _Streamed(w_branch.reshape(depth, N_BRANCH * BW, D_MODEL)), _Streamed(w_out),
    ]
    ffn_params = [
        ffn_rows, _Streamed(ffn_w_up), ffn_w_conv,
        _Streamed(ffn_w_down), _Streamed(ple_w_gate), _Streamed(ple_w_proj),
    ]

    for i in range(depth):
        x = _mixer_call(x, i, mixer_params)
        x = _ffn_call(x, p, i, ffn_params, final_norm=(i == depth - 1))
    return x
```

```python
import functools
from typing import NamedTuple

import jax
import jax.numpy as jnp
from jax import lax
from jax.experimental import pallas as pl
from jax.experimental.pallas import tpu as pltpu

D_MODEL = 1024
PLE_DIM = 256
N_BRANCH = 4
BW = 256
N_GROUPS = 4
GROUP_DIM = BW // N_GROUPS
GROUP_SHIFT = GROUP_DIM.bit_length() - 1
CHUNK = 128
CONF_K = 31
SHORT_K = 3
FFN_K = 3
D_FF = 2816
POOL_WINDOWS = (2, 4, 8, 16)
EPS = 1e-6

A_OFF, B_OFF, C_OFF, D_OFF, G_OFF = 0, 2 * BW, 4 * BW, 7 * BW, 8 * BW
IN_COLS = G_OFF + N_BRANCH * D_MODEL

SUBLANES = 8
LANES = 128
MIX_TILES = BW // LANES
TM_MIX = 512
TM_FFN = 512
GATE_DTYPE = jnp.float32
CONF_HALO = 32
SHORT_HALO = 8
POOL_HALO = 24
assert POOL_HALO - SUBLANES >= max(POOL_WINDOWS) - 2
assert LANES == 2 * GROUP_DIM and len(POOL_WINDOWS) == 2 * MIX_TILES
FFN_COLS = 256
DOWN_GROUP = 2
DOWN_LAG = 6
PLE_ROW_BLOCKS = 4
STREAM_ROWS = 128
VMEM_LIMIT_BYTES = 56 * 1024 * 1024

BF16 = jnp.bfloat16
F32 = jnp.float32


def _dot(a, b):
    return jnp.dot(a, b, preferred_element_type=F32)


def _rms_norm(x, g):
    ms = jnp.mean(x * x, axis=-1, keepdims=True)
    return x * lax.rsqrt(ms + EPS) * g


def _stream_to_bf16(jobs):
    def scoped(*scratch):
        stages, sems = scratch[0::2], scratch[1::2]

        def copy(k, i, slot):
            src, _, rows = jobs[k]
            return pltpu.make_async_copy(src.at[pl.ds(i * rows, rows), :],
                                         stages[k].at[slot], sems[k].at[slot])

        copy(0, 0, 0).start()
        for k, (src, dst, rows) in enumerate(jobs):
            n_chunks = src.shape[0] // rows
            assert n_chunks * rows == src.shape[0]
            if k + 1 < len(jobs):
                copy(k + 1, 0, 0).start()

            def body(i, carry, k=k, dst=dst, rows=rows, n_chunks=n_chunks):
                slot = lax.rem(i, 2)

                @pl.when(i + 1 < n_chunks)
                def _():
                    copy(k, i + 1, 1 - slot).start()

                copy(k, i, slot).wait()
                dst[pl.ds(pl.multiple_of(i * rows, rows), rows), :] = (
                    stages[k][slot].astype(BF16))
                return carry

            lax.fori_loop(0, n_chunks, body, 0)

    scratch = []
    for src, _, rows in jobs:
        scratch += [pltpu.VMEM((2, rows, src.shape[1]), F32), pltpu.SemaphoreType.DMA((2,))]
    pl.run_scoped(scoped, *scratch)


def _first_step():
    return (pl.program_id(0) == 0) & (pl.program_id(1) == 0)


def _split_rows(rows_ref, widths):
    views, off = [], 0
    for w in widths:
        views.append(rows_ref.at[:, pl.ds(off, w)])
        off += w
    assert off == rows_ref.shape[1]
    return views


MIXER_ROW_WIDTHS = (D_MODEL,) + (BW,) * 6
FFN_ROW_WIDTHS = (D_MODEL, 2 * D_FF, D_MODEL, D_MODEL)


def _mixer_kernel(x_ref, rows_ref, win_hbm, ws_ref, bs_ref, cw_ref, sw_ref, pw_ref,
                  wbr_hbm, wout_hbm, o_ref,
                  win_ref, wbr_ref, wout_ref,
                  h_buf, y_buf, gate_buf, conf_buf, short_buf, pool_buf, *, layer):
    s = pl.program_id(1)
    tm = x_ref.shape[0]
    g_ref, lng_ref, lnb_ref, cb_ref, clg_ref, clb_ref, ps_ref = _split_rows(
        rows_ref, MIXER_ROW_WIDTHS)

    @pl.when(_first_step())
    def _():
        _stream_to_bf16([(win_hbm.at[layer], win_ref, STREAM_ROWS),
                         (wbr_hbm.at[layer], wbr_ref, STREAM_ROWS),
                         (wout_hbm.at[layer], wout_ref, STREAM_ROWS)])

    @pl.when(s == 0)
    def _():
        conf_buf[:, 0:CONF_HALO, :] = jnp.zeros((MIX_TILES, CONF_HALO, LANES), F32)
        short_buf[:, 0:SHORT_HALO, :] = jnp.zeros((MIX_TILES, SHORT_HALO, LANES), F32)
        pool_buf[:, 0:POOL_HALO, :] = jnp.zeros((MIX_TILES, POOL_HALO, LANES), F32)

    def store_tiles(buf, halo, value):
        for c in range(MIX_TILES):
            buf[c, halo:halo + tm, :] = value[:, c * LANES:(c + 1) * LANES]

    h_buf[...] = _rms_norm(x_ref[...], g_ref[...]).astype(BF16)

    z = _dot(h_buf[...], win_ref[:, 0:G_OFF])
    zb = z[:, B_OFF:B_OFF + 2 * BW]
    store_tiles(conf_buf, CONF_HALO, zb[:, :BW] * jax.nn.sigmoid(zb[:, BW:]))
    zc = z[:, C_OFF:C_OFF + 3 * BW]
    store_tiles(short_buf, SHORT_HALO, zc[:, BW:2 * BW] * zc[:, 2 * BW:])
    zd = z[:, D_OFF:D_OFF + BW]
    store_tiles(pool_buf, POOL_HALO, zd)
    za = jax.nn.gelu(z[:, A_OFF:A_OFF + 2 * BW])

    def merge_gate(k):
        zg = _dot(h_buf[...], win_ref[:, G_OFF + k * D_MODEL:G_OFF + (k + 1) * D_MODEL])
        gate_buf[k] = jax.nn.sigmoid(zg).astype(gate_buf.dtype)

    merge_gate(0)
    merge_gate(1)

    def causal_taps(buf, halo, w_ref):
        n_taps = w_ref.shape[0]
        base = halo - (n_taps - 1)
        outs = []
        for c in range(MIX_TILES):
            lanes = slice(c * LANES, (c + 1) * LANES)
            acc = buf[c, base:base + tm, :] * w_ref[0:1, lanes]
            for k in range(1, n_taps):
                acc = acc + buf[c, base + k:base + k + tm, :] * w_ref[k:k + 1, lanes]
            buf[c, 0:halo, :] = buf[c, tm:tm + halo, :]
            outs.append(acc)
        return jnp.concatenate(outs, axis=1)

    yb = causal_taps(conf_buf, CONF_HALO, cw_ref) + cb_ref[...]

    y_buf[2] = (zc[:, :BW] * causal_taps(short_buf, SHORT_HALO, sw_ref)).astype(BF16)

    t_pos = s * tm + lax.broadcasted_iota(jnp.int32, (tm, LANES), 0) + 1
    low = lax.broadcasted_iota(jnp.int32, (tm, LANES), 1) < GROUP_DIM
    span = POOL_HALO - SUBLANES + tm
    pooled = []
    for c in range(MIX_TILES):
        w_lo, w_hi = POOL_WINDOWS[2 * c], POOL_WINDOWS[2 * c + 1]
        carry_rows = pool_buf[c, tm:tm + POOL_HALO, :]
        w = 1
        while w < w_hi:
            doubled = (pool_buf[c, SUBLANES:SUBLANES + span, :]
                       + pool_buf[c, SUBLANES - w:SUBLANES - w + span, :])
            pool_buf[c, SUBLANES:SUBLANES + span, :] = doubled
            w *= 2
            if w == w_lo:
                lo_sum = doubled[POOL_HALO - SUBLANES:, :]
        win_sum = jnp.where(low, lo_sum, doubled[POOL_HALO - SUBLANES:, :])
        cnt = jnp.minimum(t_pos, jnp.where(low, w_lo, w_hi)).astype(F32)
        pooled.append(win_sum / cnt - zd[:, c * LANES:(c + 1) * LANES])
        pool_buf[c, 0:POOL_HALO, :] = carry_rows
    pooled = jnp.concatenate(pooled, axis=1).astype(BF16)

    y_buf[3] = (_dot(pooled, pw_ref[...]) * ps_ref[...]).astype(BF16)
    merge_gate(2)
    gi = lax.broadcasted_iota(jnp.int32, (BW, BW), 0) >> GROUP_SHIFT
    gj = lax.broadcasted_iota(jnp.int32, (BW, BW), 1) >> GROUP_SHIFT
    g_avg = jnp.where(gi == gj, 1.0 / GROUP_DIM, 0.0).astype(BF16)
    ybc = yb - _dot(yb.astype(BF16), g_avg)
    merge_gate(3)

    u = za[:, :BW]
    v = za[:, BW:]
    mu = jnp.mean(v, axis=-1, keepdims=True)
    vc = v - mu
    var = jnp.mean(vc * vc, axis=-1, keepdims=True)
    v = (vc * lax.rsqrt(var + EPS) * lng_ref[...] + lnb_ref[...]).astype(BF16)
    row = lax.broadcasted_iota(jnp.int32, (CHUNK, N_GROUPS * CHUNK), 0)
    col = lax.broadcasted_iota(jnp.int32, (CHUNK, N_GROUPS * CHUNK), 1)
    w_all = jnp.concatenate([ws_ref[g] for g in range(N_GROUPS)], axis=1).astype(BF16)
    w_cat = jnp.where((col & (CHUNK - 1)) <= row, w_all, jnp.zeros((), BF16))
    lane_group = lax.broadcasted_iota(jnp.int32, (CHUNK, BW), 1) >> GROUP_SHIFT
    for n in range(tm // CHUNK):
        vn = v[n * CHUNK:(n + 1) * CHUNK, :]
        v_bd = jnp.concatenate(
            [jnp.where(lane_group == g, vn, jnp.zeros((), BF16)) for g in range(N_GROUPS)],
            axis=0)
        mixed = _dot(w_cat, v_bd) + bs_ref[...]
        y_buf[0, n * CHUNK:(n + 1) * CHUNK, :] = (
            u[n * CHUNK:(n + 1) * CHUNK, :] * mixed).astype(BF16)
    gvar = _dot((ybc * ybc).astype(BF16), g_avg)
    ybn = ybc * lax.rsqrt(gvar + EPS) * clg_ref[...] + clb_ref[...]
    y_buf[1] = (ybn * jax.nn.sigmoid(ybn)).astype(BF16)

    merged = None
    for k in (2, 3, 0, 1):
        term = _dot(y_buf[k], wbr_ref[k * BW:(k + 1) * BW, :]) * gate_buf[k].astype(F32)
        merged = term if merged is None else merged + term
    o_ref[...] = x_ref[...] + _dot(merged.astype(BF16), wout_ref[...])


def _ffn_kernel(x_ref, p_ref, rows_ref, wup_hbm, wconv_ref, wdown_hbm, wpg_hbm, wpp_hbm, o_ref,
                wup_ref, wdown_ref, wpg_ref, wpp_ref,
                h_buf, act_buf, acc_buf, up_buf, *, layer, final_norm):
    s = pl.program_id(1)
    tm = x_ref.shape[0]
    gffn_ref, bconv_ref, gple_ref, gfin_ref = _split_rows(rows_ref, FFN_ROW_WIDTHS)

    @pl.when(_first_step())
    def _():
        _stream_to_bf16([(wup_hbm.at[layer], wup_ref, STREAM_ROWS),
                         (wdown_hbm.at[layer], wdown_ref, STREAM_ROWS),
                         (wpg_hbm.at[layer], wpg_ref, STREAM_ROWS),
                         (wpp_hbm.at[layer], wpp_ref, STREAM_ROWS)])

    @pl.when(s == 0)
    def _():
        up_buf[:, 0:SHORT_HALO, :] = jnp.zeros((2 * D_FF // LANES, SHORT_HALO, LANES), F32)

    x = x_ref[...]
    h_buf[...] = _rms_norm(x, gffn_ref[...]).astype(BF16)

    base = SHORT_HALO - (FFN_K - 1)

    def up_chunk(j):
        parts = []
        for part in range(2):
            c0 = part * D_FF + j * FFN_COLS
            up = _dot(h_buf[...], wup_ref[:, c0:c0 + FFN_COLS])
            convs = []
            for c in range(FFN_COLS // LANES):
                tile = c0 // LANES + c
                lanes = slice(tile * LANES, (tile + 1) * LANES)
                up_buf[tile, SHORT_HALO:SHORT_HALO + tm, :] = up[:, c * LANES:(c + 1) * LANES]
                conv = up_buf[tile, base:base + tm, :] * wconv_ref[0:1, lanes]
                for k in range(1, FFN_K):
                    conv = conv + up_buf[tile, base + k:base + k + tm, :] * wconv_ref[k:k + 1, lanes]
                up_buf[tile, 0:SHORT_HALO, :] = up_buf[tile, tm:tm + SHORT_HALO, :]
                convs.append(conv)
            parts.append(jnp.concatenate(convs, axis=1) + bconv_ref[:, c0:c0 + FFN_COLS])
        gate, val = parts
        act_buf[:, j * FFN_COLS:(j + 1) * FFN_COLS] = (
            gate * jax.nn.sigmoid(gate) * val).astype(BF16)

    def down_group(first, last):
        rows = slice(first * FFN_COLS, last * FFN_COLS)
        partial = _dot(act_buf[:, rows], wdown_ref[rows, :])
        acc_buf[...] = (x if first == 0 else acc_buf[...]) + partial

    n_chunks = D_FF // FFN_COLS
    groups = [(a, min(a + DOWN_GROUP, n_chunks)) for a in range(0, n_chunks, DOWN_GROUP)]
    for gi, (first, last) in enumerate(groups):
        for j in range(first, last):
            up_chunk(j)
        if gi >= DOWN_LAG:
            down_group(*groups[gi - DOWN_LAG])
    for g in groups[len(groups) - DOWN_LAG:]:
        down_group(*g)

    proj = _dot(p_ref[...].astype(BF16), wpp_ref[...])
    block = tm // PLE_ROW_BLOCKS
    for r in range(PLE_ROW_BLOCKS):
        rows = slice(r * block, (r + 1) * block)
        y = acc_buf[rows, :]
        h2 = _rms_norm(y, gple_ref[...]).astype(BF16)
        gate = jax.nn.sigmoid(_dot(h2, wpg_ref[...]))
        y = y + gate * proj[rows, :]
        if final_norm:
            y = _rms_norm(y, gfin_ref[...])
        o_ref[rows, :] = y


def _const_spec(shape, layer):
    nd = len(shape)
    return pl.BlockSpec((None,) + tuple(shape), lambda b, s: (layer,) + (0,) * nd,
                        pipeline_mode=pl.Buffered(1))


def _tile_spec(tm, width):
    return pl.BlockSpec((None, tm, width), lambda b, s: (b, s, 0))


def _compiler_params():
    return pltpu.CompilerParams(dimension_semantics=("arbitrary", "arbitrary"),
                                vmem_limit_bytes=VMEM_LIMIT_BYTES)


class _Streamed(NamedTuple):
    array: jax.Array


def _param_specs(params, layer):
    specs, weight_bufs, operands = [], [], []
    for q in params:
        if isinstance(q, _Streamed):
            specs.append(pl.BlockSpec(memory_space=pl.ANY))
            weight_bufs.append(pltpu.VMEM(q.array.shape[1:], BF16))
            operands.append(q.array)
        else:
            specs.append(_const_spec(q.shape[1:], layer))
            operands.append(q)
    return specs, weight_bufs, operands


def _mixer_call(x, layer, params):
    batch, seq, _ = x.shape
    param_specs, weight_bufs, operands = _param_specs(params, layer)
    return pl.pallas_call(
        functools.partial(_mixer_kernel, layer=layer),
        grid=(batch, seq // TM_MIX),
        in_specs=[_tile_spec(TM_MIX, D_MODEL)] + param_specs,
        out_specs=_tile_spec(TM_MIX, D_MODEL),
        out_shape=jax.ShapeDtypeStruct(x.shape, F32),
        scratch_shapes=weight_bufs + [
            pltpu.VMEM((TM_MIX, D_MODEL), BF16),
            pltpu.VMEM((N_BRANCH, TM_MIX, BW), BF16),
            pltpu.VMEM((N_BRANCH, TM_MIX, D_MODEL), GATE_DTYPE),
            pltpu.VMEM((MIX_TILES, CONF_HALO + TM_MIX, LANES), F32),
            pltpu.VMEM((MIX_TILES, SHORT_HALO + TM_MIX, LANES), F32),
            pltpu.VMEM((MIX_TILES, POOL_HALO + TM_MIX, LANES), F32),
        ],
        compiler_params=_compiler_params(),
        name=f"mixer_l{layer}",
    )(x, *operands)


def _ffn_call(x, p, layer, params, final_norm):
    batch, seq, _ = x.shape
    param_specs, weight_bufs, operands = _param_specs(params, layer)
    in_specs = ([_tile_spec(TM_FFN, D_MODEL),
                 pl.BlockSpec((None, None, TM_FFN, PLE_DIM), lambda b, s: (layer, b, s, 0))]
                + param_specs)
    return pl.pallas_call(
        functools.partial(_ffn_kernel, layer=layer, final_norm=final_norm),
        grid=(batch, seq // TM_FFN),
        in_specs=in_specs,
        out_specs=_tile_spec(TM_FFN, D_MODEL),
        out_shape=jax.ShapeDtypeStruct(x.shape, F32),
        scratch_shapes=weight_bufs + [
            pltpu.VMEM((TM_FFN, D_MODEL), BF16),
            pltpu.VMEM((TM_FFN, D_FF), BF16),
            pltpu.VMEM((TM_FFN, D_MODEL), F32),
            pltpu.VMEM((2 * D_FF // LANES, SHORT_HALO + TM_FFN, LANES), F32),
        ],
        compiler_params=_compiler_params(),
        name=f"ffn_l{layer}",
    )(x, p, *operands)


def kernel(x, p, g_mix, w_in, gmlp_ln_g, gmlp_ln_b, gmlp_w_s, gmlp_b_s, conf_w_dw, conf_b_dw, conf_ln_g, conf_ln_b, short_w, pool_w, pool_scale, w_branch, w_out, g_ffn, ffn_w_up, ffn_w_conv, ffn_b_conv, ffn_w_down, g_ple, ple_w_gate, ple_w_proj, g_final):
    depth = w_in.shape[0]
    assert x.shape[1] % TM_MIX == 0 and TM_MIX % CHUNK == 0 and x.shape[1] % TM_FFN == 0
    flat = lambda a: a.reshape(depth, -1)

    mixer_rows = jnp.concatenate(
        [g_mix, gmlp_ln_g, gmlp_ln_b, conf_b_dw, flat(conf_ln_g), flat(conf_ln_b),
         flat(pool_scale)], axis=1)[:, None, :]
    ffn_rows = jnp.concatenate(
        [g_ffn, ffn_b_conv, g_ple, jnp.broadcast_to(g_final, (depth, D_MODEL))],
        axis=1)[:, None, :]
    pool_bd = jnp.einsum('gh,lgcd->lgchd', jnp.eye(N_GROUPS, dtype=F32), pool_w)
    mixer_params = [
        mixer_rows,
        _Streamed(w_in),
        gmlp_w_s,
        jnp.repeat(gmlp_b_s.transpose(0, 2, 1), GROUP_DIM, axis=2),
        conf_w_dw,
        short_w,
        pool_bd.reshape(depth, BW, BW).astype(BF16),
        _Streamed(w_branch.reshape(depth, N_BRANCH * BW, D_MODEL)), _Streamed(w_out),
    ]
    ffn_params = [
        ffn_rows, _Streamed(ffn_w_up), ffn_w_conv,
        _Streamed(ffn_w_down), _Streamed(ple_w_gate), _Streamed(ple_w_proj),
    ]

    for i in range(depth):
        x = _mixer_call(x, i, mixer_params)
        x = _ffn_call(x, p, i, ffn_params, final_norm=(i == depth - 1))
    return x
```

```python
import functools
from typing import NamedTuple

import jax
import jax.numpy as jnp
from jax import lax
from jax.experimental import pallas as pl
from jax.experimental.pallas import tpu as pltpu

D_MODEL = 1024
PLE_DIM = 256
N_BRANCH = 4
BW = 256
N_GROUPS = 4
GROUP_DIM = BW // N_GROUPS
GROUP_SHIFT = GROUP_DIM.bit_length() - 1
CHUNK = 128
CONF_K = 31
SHORT_K = 3
FFN_K = 3
D_FF = 2816
POOL_WINDOWS = (2, 4, 8, 16)
EPS = 1e-6

A_OFF, B_OFF, C_OFF, D_OFF, G_OFF = 0, 2 * BW, 4 * BW, 7 * BW, 8 * BW
IN_COLS = G_OFF + N_BRANCH * D_MODEL

SUBLANES = 8
LANES = 128
MIX_TILES = BW // LANES
TM_MIX = 512
TM_FFN = 512
GATE_DTYPE = jnp.float32
CONF_HALO = 32
SHORT_HALO = 8
POOL_HALO = 24
assert POOL_HALO - SUBLANES >= max(POOL_WINDOWS) - 2
assert LANES == 2 * GROUP_DIM and len(POOL_WINDOWS) == 2 * MIX_TILES
FFN_COLS = 256
DOWN_GROUP = 2
DOWN_LAG = 6
PLE_ROW_BLOCKS = 4
BF16_SUBLANES = 16
STREAM_CHUNK_BYTES = 3 * 1024 * 1024
STREAM_DEPTH = 4
VMEM_LIMIT_BYTES = 56 * 1024 * 1024

BF16 = jnp.bfloat16
F32 = jnp.float32


def _dot(a, b):
    return jnp.dot(a, b, preferred_element_type=F32)


def _rms_norm(x, g):
    ms = jnp.mean(x * x, axis=-1, keepdims=True)
    return x * lax.rsqrt(ms + EPS) * g


def _stream_rows(n_rows, n_cols):
    fits = [r for r in range(BF16_SUBLANES, n_rows + 1, BF16_SUBLANES)
            if n_rows % r == 0 and r * n_cols * 4 <= STREAM_CHUNK_BYTES]
    return fits[-1]


def _stream_to_bf16(src, dst):
    rows = _stream_rows(*src.shape)
    n_chunks = src.shape[0] // rows

    def scoped(stage, sem):
        def copy(i, slot):
            return pltpu.make_async_copy(src.at[pl.ds(i * rows, rows), :], stage.at[slot],
                                         sem.at[slot])

        for j in range(min(STREAM_DEPTH - 1, n_chunks)):
            copy(j, j).start()

        def body(i, carry):
            ahead = i + (STREAM_DEPTH - 1)

            @pl.when(ahead < n_chunks)
            def _():
                copy(ahead, lax.rem(ahead, STREAM_DEPTH)).start()

            slot = lax.rem(i, STREAM_DEPTH)
            copy(i, slot).wait()
            dst[pl.ds(pl.multiple_of(i * rows, rows), rows), :] = stage[slot].astype(BF16)
            return carry

        lax.fori_loop(0, n_chunks, body, 0)

    pl.run_scoped(scoped, pltpu.VMEM((STREAM_DEPTH, rows, src.shape[1]), F32),
                  pltpu.SemaphoreType.DMA((STREAM_DEPTH,)))


def _first_step():
    return (pl.program_id(0) == 0) & (pl.program_id(1) == 0)


def _split_rows(rows_ref, widths):
    views, off = [], 0
    for w in widths:
        views.append(rows_ref.at[:, pl.ds(off, w)])
        off += w
    assert off == rows_ref.shape[1]
    return views


MIXER_ROW_WIDTHS = (D_MODEL,) + (BW,) * 6
FFN_ROW_WIDTHS = (D_MODEL, 2 * D_FF, D_MODEL, D_MODEL)


def _mixer_kernel(x_ref, rows_ref, win_hbm, ws_ref, bs_ref, cw_ref, sw_ref, pw_ref,
                  wbr_hbm, wout_hbm, o_ref,
                  win_ref, wbr_ref, wout_ref,
                  h_buf, y_buf, gate_buf, conf_buf, short_buf, pool_buf, *, layer):
    s = pl.program_id(1)
    tm = x_ref.shape[0]
    g_ref, lng_ref, lnb_ref, cb_ref, clg_ref, clb_ref, ps_ref = _split_rows(
        rows_ref, MIXER_ROW_WIDTHS)

    @pl.when(_first_step())
    def _():
        _stream_to_bf16(win_hbm.at[layer], win_ref)
        _stream_to_bf16(wbr_hbm.at[layer], wbr_ref)
        _stream_to_bf16(wout_hbm.at[layer], wout_ref)

    @pl.when(s == 0)
    def _():
        conf_buf[:, 0:CONF_HALO, :] = jnp.zeros((MIX_TILES, CONF_HALO, LANES), F32)
        short_buf[:, 0:SHORT_HALO, :] = jnp.zeros((MIX_TILES, SHORT_HALO, LANES), F32)
        pool_buf[:, 0:POOL_HALO, :] = jnp.zeros((MIX_TILES, POOL_HALO, LANES), F32)

    def store_tiles(buf, halo, value):
        for c in range(MIX_TILES):
            buf[c, halo:halo + tm, :] = value[:, c * LANES:(c + 1) * LANES]

    h_buf[...] = _rms_norm(x_ref[...], g_ref[...]).astype(BF16)

    z = _dot(h_buf[...], win_ref[:, 0:G_OFF])
    zb = z[:, B_OFF:B_OFF + 2 * BW]
    store_tiles(conf_buf, CONF_HALO, zb[:, :BW] * jax.nn.sigmoid(zb[:, BW:]))
    zc = z[:, C_OFF:C_OFF + 3 * BW]
    store_tiles(short_buf, SHORT_HALO, zc[:, BW:2 * BW] * zc[:, 2 * BW:])
    zd = z[:, D_OFF:D_OFF + BW]
    store_tiles(pool_buf, POOL_HALO, zd)
    za = jax.nn.gelu(z[:, A_OFF:A_OFF + 2 * BW])

    def merge_gate(k):
        zg = _dot(h_buf[...], win_ref[:, G_OFF + k * D_MODEL:G_OFF + (k + 1) * D_MODEL])
        gate_buf[k] = jax.nn.sigmoid(zg).astype(gate_buf.dtype)

    merge_gate(0)
    merge_gate(1)

    def causal_taps(buf, halo, w_ref):
        n_taps = w_ref.shape[0]
        base = halo - (n_taps - 1)
        outs = []
        for c in range(MIX_TILES):
            lanes = slice(c * LANES, (c + 1) * LANES)
            acc = buf[c, base:base + tm, :] * w_ref[0:1, lanes]
            for k in range(1, n_taps):
                acc = acc + buf[c, base + k:base + k + tm, :] * w_ref[k:k + 1, lanes]
            buf[c, 0:halo, :] = buf[c, tm:tm + halo, :]
            outs.append(acc)
        return jnp.concatenate(outs, axis=1)

    yb = causal_taps(conf_buf, CONF_HALO, cw_ref) + cb_ref[...]

    y_buf[2] = (zc[:, :BW] * causal_taps(short_buf, SHORT_HALO, sw_ref)).astype(BF16)

    t_pos = s * tm + lax.broadcasted_iota(jnp.int32, (tm, LANES), 0) + 1
    low = lax.broadcasted_iota(jnp.int32, (tm, LANES), 1) < GROUP_DIM
    span = POOL_HALO - SUBLANES + tm
    pooled = []
    for c in range(MIX_TILES):
        w_lo, w_hi = POOL_WINDOWS[2 * c], POOL_WINDOWS[2 * c + 1]
        carry_rows = pool_buf[c, tm:tm + POOL_HALO, :]
        w = 1
        while w < w_hi:
            doubled = (pool_buf[c, SUBLANES:SUBLANES + span, :]
                       + pool_buf[c, SUBLANES - w:SUBLANES - w + span, :])
            pool_buf[c, SUBLANES:SUBLANES + span, :] = doubled
            w *= 2
            if w == w_lo:
                lo_sum = doubled[POOL_HALO - SUBLANES:, :]
        win_sum = jnp.where(low, lo_sum, doubled[POOL_HALO - SUBLANES:, :])
        cnt = jnp.minimum(t_pos, jnp.where(low, w_lo, w_hi)).astype(F32)
        pooled.append(win_sum / cnt - zd[:, c * LANES:(c + 1) * LANES])
        pool_buf[c, 0:POOL_HALO, :] = carry_rows
    pooled = jnp.concatenate(pooled, axis=1).astype(BF16)

    y_buf[3] = (_dot(pooled, pw_ref[...]) * ps_ref[...]).astype(BF16)
    merge_gate(2)
    gi = lax.broadcasted_iota(jnp.int32, (BW, BW), 0) >> GROUP_SHIFT
    gj = lax.broadcasted_iota(jnp.int32, (BW, BW), 1) >> GROUP_SHIFT
    g_avg = jnp.where(gi == gj, 1.0 / GROUP_DIM, 0.0).astype(BF16)
    ybc = yb - _dot(yb.astype(BF16), g_avg)
    merge_gate(3)

    u = za[:, :BW]
    v = za[:, BW:]
    mu = jnp.mean(v, axis=-1, keepdims=True)
    vc = v - mu
    var = jnp.mean(vc * vc, axis=-1, keepdims=True)
    v = (vc * lax.rsqrt(var + EPS) * lng_ref[...] + lnb_ref[...]).astype(BF16)
    row = lax.broadcasted_iota(jnp.int32, (CHUNK, N_GROUPS * CHUNK), 0)
    col = lax.broadcasted_iota(jnp.int32, (CHUNK, N_GROUPS * CHUNK), 1)
    w_all = jnp.concatenate([ws_ref[g] for g in range(N_GROUPS)], axis=1).astype(BF16)
    w_cat = jnp.where((col & (CHUNK - 1)) <= row, w_all, jnp.zeros((), BF16))
    lane_group = lax.broadcasted_iota(jnp.int32, (CHUNK, BW), 1) >> GROUP_SHIFT
    for n in range(tm // CHUNK):
        vn = v[n * CHUNK:(n + 1) * CHUNK, :]
        v_bd = jnp.concatenate(
            [jnp.where(lane_group == g, vn, jnp.zeros((), BF16)) for g in range(N_GROUPS)],
            axis=0)
        mixed = _dot(w_cat, v_bd) + bs_ref[...]
        y_buf[0, n * CHUNK:(n + 1) * CHUNK, :] = (
            u[n * CHUNK:(n + 1) * CHUNK, :] * mixed).astype(BF16)
    gvar = _dot((ybc * ybc).astype(BF16), g_avg)
    ybn = ybc * lax.rsqrt(gvar + EPS) * clg_ref[...] + clb_ref[...]
    y_buf[1] = (ybn * jax.nn.sigmoid(ybn)).astype(BF16)

    merged = None
    for k in (2, 3, 0, 1):
        term = _dot(y_buf[k], wbr_ref[k * BW:(k + 1) * BW, :]) * gate_buf[k].astype(F32)
        merged = term if merged is None else merged + term
    o_ref[...] = x_ref[...] + _dot(merged.astype(BF16), wout_ref[...])


def _ffn_kernel(x_ref, p_ref, rows_ref, wup_hbm, wconv_ref, wdown_hbm, wpg_hbm, wpp_hbm, o_ref,
                wup_ref, wdown_ref, wpg_ref, wpp_ref,
                h_buf, act_buf, acc_buf, up_buf, *, layer, final_norm):
    s = pl.program_id(1)
    tm = x_ref.shape[0]
    gffn_ref, bconv_ref, gple_ref, gfin_ref = _split_rows(rows_ref, FFN_ROW_WIDTHS)

    @pl.when(_first_step())
    def _():
        _stream_to_bf16(wup_hbm.at[layer], wup_ref)
        _stream_to_bf16(wdown_hbm.at[layer], wdown_ref)
        _stream_to_bf16(wpg_hbm.at[layer], wpg_ref)
        _stream_to_bf16(wpp_hbm.at[layer], wpp_ref)

    @pl.when(s == 0)
    def _():
        up_buf[:, 0:SHORT_HALO, :] = jnp.zeros((2 * D_FF // LANES, SHORT_HALO, LANES), F32)

    x = x_ref[...]
    h_buf[...] = _rms_norm(x, gffn_ref[...]).astype(BF16)

    base = SHORT_HALO - (FFN_K - 1)

    def up_chunk(j):
        parts = []
        for part in range(2):
            c0 = part * D_FF + j * FFN_COLS
            up = _dot(h_buf[...], wup_ref[:, c0:c0 + FFN_COLS])
            convs = []
            for c in range(FFN_COLS // LANES):
                tile = c0 // LANES + c
                lanes = slice(tile * LANES, (tile + 1) * LANES)
                up_buf[tile, SHORT_HALO:SHORT_HALO + tm, :] = up[:, c * LANES:(c + 1) * LANES]
                conv = up_buf[tile, base:base + tm, :] * wconv_ref[0:1, lanes]
                for k in range(1, FFN_K):
                    conv = conv + up_buf[tile, base + k:base + k + tm, :] * wconv_ref[k:k + 1, lanes]
                up_buf[tile, 0:SHORT_HALO, :] = up_buf[tile, tm:tm + SHORT_HALO, :]
                convs.append(conv)
            parts.append(jnp.concatenate(convs, axis=1) + bconv_ref[:, c0:c0 + FFN_COLS])
        gate, val = parts
        act_buf[:, j * FFN_COLS:(j + 1) * FFN_COLS] = (
            gate * jax.nn.sigmoid(gate) * val).astype(BF16)

    def down_group(first, last):
        rows = slice(first * FFN_COLS, last * FFN_COLS)
        partial = _dot(act_buf[:, rows], wdown_ref[rows, :])
        acc_buf[...] = (x if first == 0 else acc_buf[...]) + partial

    n_chunks = D_FF // FFN_COLS
    groups = [(a, min(a + DOWN_GROUP, n_chunks)) for a in range(0, n_chunks, DOWN_GROUP)]
    for gi, (first, last) in enumerate(groups):
        for j in range(first, last):
            up_chunk(j)
        if gi >= DOWN_LAG:
            down_group(*groups[gi - DOWN_LAG])
    for g in groups[len(groups) - DOWN_LAG:]:
        down_group(*g)

    proj = _dot(p_ref[...].astype(BF16), wpp_ref[...])
    block = tm // PLE_ROW_BLOCKS
    for r in range(PLE_ROW_BLOCKS):
        rows = slice(r * block, (r + 1) * block)
        y = acc_buf[rows, :]
        h2 = _rms_norm(y, gple_ref[...]).astype(BF16)
        gate = jax.nn.sigmoid(_dot(h2, wpg_ref[...]))
        y = y + gate * proj[rows, :]
        if final_norm:
            y = _rms_norm(y, gfin_ref[...])
        o_ref[rows, :] = y


def _const_spec(shape, layer):
    nd = len(shape)
    return pl.BlockSpec((None,) + tuple(shape), lambda b, s: (layer,) + (0,) * nd,
                        pipeline_mode=pl.Buffered(1))


def _tile_spec(tm, width):
    return pl.BlockSpec((None, tm, width), lambda b, s: (b, s, 0))


def _compiler_params():
    return pltpu.CompilerParams(dimension_semantics=("arbitrary", "arbitrary"),
                                vmem_limit_bytes=VMEM_LIMIT_BYTES)


class _Streamed(NamedTuple):
    array: jax.Array


def _param_specs(params, layer):
    specs, weight_bufs, operands = [], [], []
    for q in params:
        if isinstance(q, _Streamed):
            specs.append(pl.BlockSpec(memory_space=pl.ANY))
            weight_bufs.append(pltpu.VMEM(q.array.shape[1:], BF16))
            operands.append(q.array)
        else:
            specs.append(_const_spec(q.shape[1:], layer))
            operands.append(q)
    return specs, weight_bufs, operands


def _mixer_call(x, layer, params):
    batch, seq, _ = x.shape
    param_specs, weight_bufs, operands = _param_specs(params, layer)
    return pl.pallas_call(
        functools.partial(_mixer_kernel, layer=layer),
        grid=(batch, seq // TM_MIX),
        in_specs=[_tile_spec(TM_MIX, D_MODEL)] + param_specs,
        out_specs=_tile_spec(TM_MIX, D_MODEL),
        out_shape=jax.ShapeDtypeStruct(x.shape, F32),
        scratch_shapes=weight_bufs + [
            pltpu.VMEM((TM_MIX, D_MODEL), BF16),
            pltpu.VMEM((N_BRANCH, TM_MIX, BW), BF16),
            pltpu.VMEM((N_BRANCH, TM_MIX, D_MODEL), GATE_DTYPE),
            pltpu.VMEM((MIX_TILES, CONF_HALO + TM_MIX, LANES), F32),
            pltpu.VMEM((MIX_TILES, SHORT_HALO + TM_MIX, LANES), F32),
            pltpu.VMEM((MIX_TILES, POOL_HALO + TM_MIX, LANES), F32),
        ],
        compiler_params=_compiler_params(),
        name=f"mixer_l{layer}",
    )(x, *operands)


def _ffn_call(x, p, layer, params, final_norm):
    batch, seq, _ = x.shape
    param_specs, weight_bufs, operands = _param_specs(params, layer)
    in_specs = ([_tile_spec(TM_FFN, D_MODEL),
                 pl.BlockSpec((None, None, TM_FFN, PLE_DIM), lambda b, s: (layer, b, s, 0))]
                + param_specs)
    return pl.pallas_call(
        functools.partial(_ffn_kernel, layer=layer, final_norm=final_norm),
        grid=(batch, seq // TM_FFN),
        in_specs=in_specs,
        out_specs=_tile_spec(TM_FFN, D_MODEL),
        out_shape=jax.ShapeDtypeStruct(x.shape, F32),
        scratch_shapes=weight_bufs + [
            pltpu.VMEM((TM_FFN, D_MODEL), BF16),
            pltpu.VMEM((TM_FFN, D_FF), BF16),
            pltpu.VMEM((TM_FFN, D_MODEL), F32),
            pltpu.VMEM((2 * D_FF // LANES, SHORT_HALO + TM_FFN, LANES), F32),
        ],
        compiler_params=_compiler_params(),
        name=f"ffn_l{layer}",
    )(x, p, *operands)


def kernel(x, p, g_mix, w_in, gmlp_ln_g, gmlp_ln_b, gmlp_w_s, gmlp_b_s, conf_w_dw, conf_b_dw, conf_ln_g, conf_ln_b, short_w, pool_w, pool_scale, w_branch, w_out, g_ffn, ffn_w_up, ffn_w_conv, ffn_b_conv, ffn_w_down, g_ple, ple_w_gate, ple_w_proj, g_final):
    depth = w_in.shape[0]
    assert x.shape[1] % TM_MIX == 0 and TM_MIX % CHUNK == 0 and x.shape[1] % TM_FFN == 0
    flat = lambda a: a.reshape(depth, -1)

    mixer_rows = jnp.concatenate(
        [g_mix, gmlp_ln_g, gmlp_ln_b, conf_b_dw, flat(conf_ln_g), flat(conf_ln_b),
         flat(pool_scale)], axis=1)[:, None, :]
    ffn_rows = jnp.concatenate(
        [g_ffn, ffn_b_conv, g_ple, jnp.broadcast_to(g_final, (depth, D_MODEL))],
        axis=1)[:, None, :]
    pool_bd = jnp.einsum('gh,lgcd->lgchd', jnp.eye(N_GROUPS, dtype=F32), pool_w)
    mixer_params = [
        mixer_rows,
        _Streamed(w_in),
        gmlp_w_s,
        jnp.repeat(gmlp_b_s.transpose(0, 2, 1), GROUP_DIM, axis=2),
        conf_w_dw,
        short_w,
        pool_bd.reshape(depth, BW, BW).astype(BF16),
        _Streamed(w_branch.reshape(depth, N_BRANCH * BW, D_MODEL)), _Streamed(w_out),
    ]
    ffn_params = [
        ffn_rows, _Streamed(ffn_w_up), ffn_w_conv,
        _Streamed(ffn_w_down), _Streamed(ple_w_gate), _Streamed(ple_w_proj),
    ]

    for i in range(depth):
        x = _mixer_call(x, i, mixer_params)
        x = _ffn_call(x, p, i, ffn_params, final_norm=(i == depth - 1))
    return x
```

```python
import functools
from typing import NamedTuple

import jax
import jax.numpy as jnp
from jax import lax
from jax.experimental import pallas as pl
from jax.experimental.pallas import tpu as pltpu

D_MODEL = 1024
PLE_DIM = 256
N_BRANCH = 4
BW = 256
N_GROUPS = 4
GROUP_DIM = BW // N_GROUPS
GROUP_SHIFT = GROUP_DIM.bit_length() - 1
CHUNK = 128
CONF_K = 31
SHORT_K = 3
FFN_K = 3
D_FF = 2816
POOL_WINDOWS = (2, 4, 8, 16)
EPS = 1e-6

A_OFF, B_OFF, C_OFF, D_OFF, G_OFF = 0, 2 * BW, 4 * BW, 7 * BW, 8 * BW
IN_COLS = G_OFF + N_BRANCH * D_MODEL

SUBLANES = 8
LANES = 128
MIX_TILES = BW // LANES
TM_MIX = 512
TM_FFN = 512
CONF_HALO = 32
SHORT_HALO = 8
POOL_HALO = 24
assert POOL_HALO - SUBLANES >= max(POOL_WINDOWS) - 2
assert LANES == 2 * GROUP_DIM and len(POOL_WINDOWS) == 2 * MIX_TILES
FFN_COLS = 256
DOWN_GROUP = 2
PLE_ROW_BLOCKS = 4
BF16_SUBLANES = 16
STREAM_CHUNK_BYTES = 3 * 1024 * 1024
STREAM_DEPTH = 3
VMEM_LIMIT_BYTES = 56 * 1024 * 1024

BF16 = jnp.bfloat16
F32 = jnp.float32


def _dot(a, b):
    return jnp.dot(a, b, preferred_element_type=F32)


def _rms_norm(x, g):
    ms = jnp.mean(x * x, axis=-1, keepdims=True)
    return x * lax.rsqrt(ms + EPS) * g


def _stream_rows(n_rows, n_cols):
    fits = [r for r in range(BF16_SUBLANES, n_rows + 1, BF16_SUBLANES)
            if n_rows % r == 0 and r * n_cols * 4 <= STREAM_CHUNK_BYTES]
    return fits[-1]


def _stream_to_bf16(src, dst):
    rows = _stream_rows(*src.shape)
    n_chunks = src.shape[0] // rows

    def scoped(stage, sem):
        def copy(i, slot):
            return pltpu.make_async_copy(src.at[pl.ds(i * rows, rows), :], stage.at[slot],
                                         sem.at[slot])

        for j in range(min(STREAM_DEPTH - 1, n_chunks)):
            copy(j, j).start()

        def body(i, carry):
            ahead = i + (STREAM_DEPTH - 1)

            @pl.when(ahead < n_chunks)
            def _():
                copy(ahead, lax.rem(ahead, STREAM_DEPTH)).start()

            slot = lax.rem(i, STREAM_DEPTH)
            copy(i, slot).wait()
            dst[pl.ds(pl.multiple_of(i * rows, rows), rows), :] = stage[slot].astype(BF16)
            return carry

        lax.fori_loop(0, n_chunks, body, 0)

    pl.run_scoped(scoped, pltpu.VMEM((STREAM_DEPTH, rows, src.shape[1]), F32),
                  pltpu.SemaphoreType.DMA((STREAM_DEPTH,)))


def _first_step():
    return (pl.program_id(0) == 0) & (pl.program_id(1) == 0)


def _layer_rows(layer, *refs):
    return [r.at[pl.ds(layer, 1), :] for r in refs]


def _mixer_kernel(x_ref, g_all, lng_all, lnb_all, cb_all, clg_all, clb_all, ps_all,
                  win_hbm, ws_ref, bs_ref, cw_ref, sw_ref, pw_ref,
                  wbr_hbm, wout_hbm, o_ref,
                  win_ref, wbr_ref, wout_ref,
                  h_buf, y_buf, gate_buf, conf_buf, short_buf, pool_buf, *, layer):
    s = pl.program_id(1)
    tm = x_ref.shape[0]
    g_ref, lng_ref, lnb_ref, cb_ref, clg_ref, clb_ref, ps_ref = _layer_rows(
        layer, g_all, lng_all, lnb_all, cb_all, clg_all, clb_all, ps_all)

    @pl.when(_first_step())
    def _():
        _stream_to_bf16(win_hbm.at[layer], win_ref)
        _stream_to_bf16(wbr_hbm.at[layer], wbr_ref)
        _stream_to_bf16(wout_hbm.at[layer], wout_ref)

    @pl.when(s == 0)
    def _():
        conf_buf[:, 0:CONF_HALO, :] = jnp.zeros((MIX_TILES, CONF_HALO, LANES), F32)
        short_buf[:, 0:SHORT_HALO, :] = jnp.zeros((MIX_TILES, SHORT_HALO, LANES), F32)
        pool_buf[:, 0:POOL_HALO, :] = jnp.zeros((MIX_TILES, POOL_HALO, LANES), F32)

    def store_tiles(buf, halo, value):
        for c in range(MIX_TILES):
            buf[c, halo:halo + tm, :] = value[:, c * LANES:(c + 1) * LANES]

    h_buf[...] = _rms_norm(x_ref[...], g_ref[...]).astype(BF16)

    z = _dot(h_buf[...], win_ref[:, 0:G_OFF])
    zb = z[:, B_OFF:B_OFF + 2 * BW]
    store_tiles(conf_buf, CONF_HALO, zb[:, :BW] * jax.nn.sigmoid(zb[:, BW:]))
    zc = z[:, C_OFF:C_OFF + 3 * BW]
    store_tiles(short_buf, SHORT_HALO, zc[:, BW:2 * BW] * zc[:, 2 * BW:])
    zd = z[:, D_OFF:D_OFF + BW]
    store_tiles(pool_buf, POOL_HALO, zd)
    za = jax.nn.gelu(z[:, A_OFF:A_OFF + 2 * BW])

    def merge_gate(k):
        zg = _dot(h_buf[...], win_ref[:, G_OFF + k * D_MODEL:G_OFF + (k + 1) * D_MODEL])
        gate_buf[k] = jax.nn.sigmoid(zg)

    merge_gate(0)
    merge_gate(1)

    def causal_taps(buf, halo, w_ref):
        n_taps = w_ref.shape[0]
        base = halo - (n_taps - 1)
        outs = []
        for c in range(MIX_TILES):
            lanes = slice(c * LANES, (c + 1) * LANES)
            acc = buf[c, base:base + tm, :] * w_ref[0:1, lanes]
            for k in range(1, n_taps):
                acc = acc + buf[c, base + k:base + k + tm, :] * w_ref[k:k + 1, lanes]
            buf[c, 0:halo, :] = buf[c, tm:tm + halo, :]
            outs.append(acc)
        return jnp.concatenate(outs, axis=1)

    yb = causal_taps(conf_buf, CONF_HALO, cw_ref) + cb_ref[...]

    y_buf[2] = (zc[:, :BW] * causal_taps(short_buf, SHORT_HALO, sw_ref)).astype(BF16)

    t_pos = s * tm + lax.broadcasted_iota(jnp.int32, (tm, LANES), 0) + 1
    low = lax.broadcasted_iota(jnp.int32, (tm, LANES), 1) < GROUP_DIM
    span = POOL_HALO - SUBLANES + tm
    pooled = []
    for c in range(MIX_TILES):
        w_lo, w_hi = POOL_WINDOWS[2 * c], POOL_WINDOWS[2 * c + 1]
        carry_rows = pool_buf[c, tm:tm + POOL_HALO, :]
        w = 1
        while w < w_hi:
            doubled = (pool_buf[c, SUBLANES:SUBLANES + span, :]
                       + pool_buf[c, SUBLANES - w:SUBLANES - w + span, :])
            pool_buf[c, SUBLANES:SUBLANES + span, :] = doubled
            w *= 2
            if w == w_lo:
                lo_sum = doubled[POOL_HALO - SUBLANES:, :]
        win_sum = jnp.where(low, lo_sum, doubled[POOL_HALO - SUBLANES:, :])
        cnt = jnp.minimum(t_pos, jnp.where(low, w_lo, w_hi)).astype(F32)
        pooled.append(win_sum / cnt - zd[:, c * LANES:(c + 1) * LANES])
        pool_buf[c, 0:POOL_HALO, :] = carry_rows
    pooled = jnp.concatenate(pooled, axis=1).astype(BF16)

    y_buf[3] = (_dot(pooled, pw_ref[...]) * ps_ref[...]).astype(BF16)
    merge_gate(2)
    gi = lax.broadcasted_iota(jnp.int32, (BW, BW), 0) >> GROUP_SHIFT
    gj = lax.broadcasted_iota(jnp.int32, (BW, BW), 1) >> GROUP_SHIFT
    g_avg = jnp.where(gi == gj, 1.0 / GROUP_DIM, 0.0).astype(BF16)
    ybc = yb - _dot(yb.astype(BF16), g_avg)
    merge_gate(3)

    u = za[:, :BW]
    v = za[:, BW:]
    mu = jnp.mean(v, axis=-1, keepdims=True)
    vc = v - mu
    var = jnp.mean(vc * vc, axis=-1, keepdims=True)
    v = (vc * lax.rsqrt(var + EPS) * lng_ref[...] + lnb_ref[...]).astype(BF16)
    row = lax.broadcasted_iota(jnp.int32, (CHUNK, N_GROUPS * CHUNK), 0)
    col = lax.broadcasted_iota(jnp.int32, (CHUNK, N_GROUPS * CHUNK), 1)
    w_all = jnp.concatenate([ws_ref[g] for g in range(N_GROUPS)], axis=1).astype(BF16)
    w_cat = jnp.where((col & (CHUNK - 1)) <= row, w_all, jnp.zeros((), BF16))
    lane_group = lax.broadcasted_iota(jnp.int32, (CHUNK, BW), 1) >> GROUP_SHIFT
    for n in range(tm // CHUNK):
        vn = v[n * CHUNK:(n + 1) * CHUNK, :]
        v_bd = jnp.concatenate(
            [jnp.where(lane_group == g, vn, jnp.zeros((), BF16)) for g in range(N_GROUPS)],
            axis=0)
        mixed = _dot(w_cat, v_bd) + bs_ref[...]
        y_buf[0, n * CHUNK:(n + 1) * CHUNK, :] = (
            u[n * CHUNK:(n + 1) * CHUNK, :] * mixed).astype(BF16)
    gvar = _dot((ybc * ybc).astype(BF16), g_avg)
    ybn = ybc * lax.rsqrt(gvar + EPS) * clg_ref[...] + clb_ref[...]
    y_buf[1] = (ybn * jax.nn.sigmoid(ybn)).astype(BF16)

    merged = None
    for k in (2, 3, 0, 1):
        term = _dot(y_buf[k], wbr_ref[k * BW:(k + 1) * BW, :]) * gate_buf[k]
        merged = term if merged is None else merged + term
    o_ref[...] = x_ref[...] + _dot(merged.astype(BF16), wout_ref[...])


def _ffn_kernel(x_ref, p_ref, gffn_all, bconv_all, gple_all, gfin_ref,
                wup_hbm, wconv_ref, wdown_hbm, wpg_hbm, wpp_hbm, o_ref,
                wup_ref, wdown_ref, wpg_ref, wpp_ref,
                h_buf, act_buf, acc_buf, up_buf, *, layer, final_norm):
    s = pl.program_id(1)
    tm = x_ref.shape[0]
    gffn_ref, bconv_ref, gple_ref = _layer_rows(layer, gffn_all, bconv_all, gple_all)

    @pl.when(_first_step())
    def _():
        _stream_to_bf16(wup_hbm.at[layer], wup_ref)
        _stream_to_bf16(wdown_hbm.at[layer], wdown_ref)
        _stream_to_bf16(wpg_hbm.at[layer], wpg_ref)
        _stream_to_bf16(wpp_hbm.at[layer], wpp_ref)

    @pl.when(s == 0)
    def _():
        up_buf[:, 0:SHORT_HALO, :] = jnp.zeros((2 * D_FF // LANES, SHORT_HALO, LANES), F32)

    x = x_ref[...]
    h_buf[...] = _rms_norm(x, gffn_ref[...]).astype(BF16)

    base = SHORT_HALO - (FFN_K - 1)

    def up_chunk(j):
        parts = []
        for part in range(2):
            c0 = part * D_FF + j * FFN_COLS
            up = _dot(h_buf[...], wup_ref[:, c0:c0 + FFN_COLS])
            convs = []
            for c in range(FFN_COLS // LANES):
                tile = c0 // LANES + c
                lanes = slice(tile * LANES, (tile + 1) * LANES)
                up_buf[tile, SHORT_HALO:SHORT_HALO + tm, :] = up[:, c * LANES:(c + 1) * LANES]
                conv = up_buf[tile, base:base + tm, :] * wconv_ref[0:1, lanes]
                for k in range(1, FFN_K):
                    conv = conv + up_buf[tile, base + k:base + k + tm, :] * wconv_ref[k:k + 1, lanes]
                up_buf[tile, 0:SHORT_HALO, :] = up_buf[tile, tm:tm + SHORT_HALO, :]
                convs.append(conv)
            parts.append(jnp.concatenate(convs, axis=1) + bconv_ref[:, c0:c0 + FFN_COLS])
        gate, val = parts
        act_buf[:, j * FFN_COLS:(j + 1) * FFN_COLS] = (
            gate * jax.nn.sigmoid(gate) * val).astype(BF16)

    def down_group(first, last):
        rows = slice(first * FFN_COLS, last * FFN_COLS)
        partial = _dot(act_buf[:, rows], wdown_ref[rows, :])
        acc_buf[...] = (x if first == 0 else acc_buf[...]) + partial

    n_chunks = D_FF // FFN_COLS
    for j in range(n_chunks):
        up_chunk(j)
    for first in range(0, n_chunks, DOWN_GROUP):
        down_group(first, min(first + DOWN_GROUP, n_chunks))

    proj = _dot(p_ref[...].astype(BF16), wpp_ref[...])
    block = tm // PLE_ROW_BLOCKS
    for r in range(PLE_ROW_BLOCKS):
        rows = slice(r * block, (r + 1) * block)
        y = acc_buf[rows, :]
        h2 = _rms_norm(y, gple_ref[...]).astype(BF16)
        gate = jax.nn.sigmoid(_dot(h2, wpg_ref[...]))
        y = y + gate * proj[rows, :]
        if final_norm:
            y = _rms_norm(y, gfin_ref[...])
        o_ref[rows, :] = y


def _const_spec(shape, layer):
    nd = len(shape)
    return pl.BlockSpec((None,) + tuple(shape), lambda b, s: (layer,) + (0,) * nd,
                        pipeline_mode=pl.Buffered(1))


def _tile_spec(tm, width):
    return pl.BlockSpec((None, tm, width), lambda b, s: (b, s, 0))


def _compiler_params():
    return pltpu.CompilerParams(dimension_semantics=("arbitrary", "arbitrary"),
                                vmem_limit_bytes=VMEM_LIMIT_BYTES)


class _Streamed(NamedTuple):
    array: jax.Array


class _Whole(NamedTuple):
    array: jax.Array


def _param_specs(params, layer):
    specs, weight_bufs, operands = [], [], []
    for q in params:
        if isinstance(q, _Streamed):
            specs.append(pl.BlockSpec(memory_space=pl.ANY))
            weight_bufs.append(pltpu.VMEM(q.array.shape[1:], BF16))
            operands.append(q.array)
        elif isinstance(q, _Whole):
            specs.append(pl.BlockSpec(q.array.shape, lambda b, s, nd=q.array.ndim: (0,) * nd,
                                      pipeline_mode=pl.Buffered(1)))
            operands.append(q.array)
        else:
            specs.append(_const_spec(q.shape[1:], layer))
            operands.append(q)
    return specs, weight_bufs, operands


def _mixer_call(x, layer, params):
    batch, seq, _ = x.shape
    param_specs, weight_bufs, operands = _param_specs(params, layer)
    return pl.pallas_call(
        functools.partial(_mixer_kernel, layer=layer),
        grid=(batch, seq // TM_MIX),
        in_specs=[_tile_spec(TM_MIX, D_MODEL)] + param_specs,
        out_specs=_tile_spec(TM_MIX, D_MODEL),
        out_shape=jax.ShapeDtypeStruct(x.shape, F32),
        scratch_shapes=weight_bufs + [
            pltpu.VMEM((TM_MIX, D_MODEL), BF16),
            pltpu.VMEM((N_BRANCH, TM_MIX, BW), BF16),
            pltpu.VMEM((N_BRANCH, TM_MIX, D_MODEL), F32),
            pltpu.VMEM((MIX_TILES, CONF_HALO + TM_MIX, LANES), F32),
            pltpu.VMEM((MIX_TILES, SHORT_HALO + TM_MIX, LANES), F32),
            pltpu.VMEM((MIX_TILES, POOL_HALO + TM_MIX, LANES), F32),
        ],
        compiler_params=_compiler_params(),
        name=f"mixer_l{layer}",
    )(x, *operands)


def _ffn_call(x, p, layer, params, final_norm):
    batch, seq, _ = x.shape
    param_specs, weight_bufs, operands = _param_specs(params, layer)
    in_specs = ([_tile_spec(TM_FFN, D_MODEL),
                 pl.BlockSpec((None, None, TM_FFN, PLE_DIM), lambda b, s: (layer, b, s, 0))]
                + param_specs)
    return pl.pallas_call(
        functools.partial(_ffn_kernel, layer=layer, final_norm=final_norm),
        grid=(batch, seq // TM_FFN),
        in_specs=in_specs,
        out_specs=_tile_spec(TM_FFN, D_MODEL),
        out_shape=jax.ShapeDtypeStruct(x.shape, F32),
        scratch_shapes=weight_bufs + [
            pltpu.VMEM((TM_FFN, D_MODEL), BF16),
            pltpu.VMEM((TM_FFN, D_FF), BF16),
            pltpu.VMEM((TM_FFN, D_MODEL), F32),
            pltpu.VMEM((2 * D_FF // LANES, SHORT_HALO + TM_FFN, LANES), F32),
        ],
        compiler_params=_compiler_params(),
        name=f"ffn_l{layer}",
    )(x, p, *operands)


def kernel(x, p, g_mix, w_in, gmlp_ln_g, gmlp_ln_b, gmlp_w_s, gmlp_b_s, conf_w_dw, conf_b_dw, conf_ln_g, conf_ln_b, short_w, pool_w, pool_scale, w_branch, w_out, g_ffn, ffn_w_up, ffn_w_conv, ffn_b_conv, ffn_w_down, g_ple, ple_w_gate, ple_w_proj, g_final):
    depth = w_in.shape[0]
    assert x.shape[1] % TM_MIX == 0 and TM_MIX % CHUNK == 0 and x.shape[1] % TM_FFN == 0
    flat = lambda a: a.reshape(depth, -1)

    pool_bd = jnp.einsum('gh,lgcd->lgchd', jnp.eye(N_GROUPS, dtype=F32), pool_w)
    mixer_params = [
        _Whole(g_mix), _Whole(gmlp_ln_g), _Whole(gmlp_ln_b), _Whole(conf_b_dw),
        _Whole(flat(conf_ln_g)), _Whole(flat(conf_ln_b)), _Whole(flat(pool_scale)),
        _Streamed(w_in),
        gmlp_w_s,
        jnp.repeat(gmlp_b_s.transpose(0, 2, 1), GROUP_DIM, axis=2),
        conf_w_dw,
        short_w,
        pool_bd.reshape(depth, BW, BW).astype(BF16),
        _Streamed(w_branch.reshape(depth, N_BRANCH * BW, D_MODEL)), _Streamed(w_out),
    ]
    ffn_params = [
        _Whole(g_ffn), _Whole(ffn_b_conv), _Whole(g_ple), _Whole(g_final.reshape(1, D_MODEL)),
        _Streamed(ffn_w_up), ffn_w_conv,
        _Streamed(ffn_w_down), _Streamed(ple_w_gate), _Streamed(ple_w_proj),
    ]

    for i in range(depth):
        x = _mixer_call(x, i, mixer_params)
        x = _ffn_call(x, p, i, ffn_params, final_norm=(i == depth - 1))
    return x
```

```python
import functools
from typing import NamedTuple

import jax
import jax.numpy as jnp
from jax import lax
from jax.experimental import pallas as pl
from jax.experimental.pallas import tpu as pltpu

D_MODEL = 1024
PLE_DIM = 256
N_BRANCH = 4
BW = 256
N_GROUPS = 4
GROUP_DIM = BW // N_GROUPS
GROUP_SHIFT = GROUP_DIM.bit_length() - 1
CHUNK = 128
CONF_K = 31
SHORT_K = 3
FFN_K = 3
D_FF = 2816
POOL_WINDOWS = (2, 4, 8, 16)
EPS = 1e-6

A_OFF, B_OFF, C_OFF, D_OFF, G_OFF = 0, 2 * BW, 4 * BW, 7 * BW, 8 * BW
IN_COLS = G_OFF + N_BRANCH * D_MODEL

SUBLANES = 8
LANES = 128
MIX_TILES = BW // LANES
TM_MIX = 512
TM_FFN = 512
CONF_HALO = 32
SHORT_HALO = 8
POOL_HALO = 24
assert POOL_HALO - SUBLANES >= max(POOL_WINDOWS) - 2
assert LANES == 2 * GROUP_DIM and len(POOL_WINDOWS) == 2 * MIX_TILES
FFN_COLS = 256
DOWN_GROUP = 4
PLE_ROW_BLOCKS = 4
BF16_SUBLANES = 16
STREAM_CHUNK_BYTES = 3 * 1024 * 1024
STREAM_DEPTH = 3
VMEM_LIMIT_BYTES = 56 * 1024 * 1024

BF16 = jnp.bfloat16
F32 = jnp.float32


def _dot(a, b):
    return jnp.dot(a, b, preferred_element_type=F32)


def _rms_norm(x, g):
    ms = jnp.mean(x * x, axis=-1, keepdims=True)
    return x * lax.rsqrt(ms + EPS) * g


def _stream_rows(n_rows, n_cols):
    fits = [r for r in range(BF16_SUBLANES, n_rows + 1, BF16_SUBLANES)
            if n_rows % r == 0 and r * n_cols * 4 <= STREAM_CHUNK_BYTES]
    return fits[-1]


def _stream_to_bf16(src, dst):
    rows = _stream_rows(*src.shape)
    n_chunks = src.shape[0] // rows

    def scoped(stage, sem):
        def copy(i, slot):
            return pltpu.make_async_copy(src.at[pl.ds(i * rows, rows), :], stage.at[slot],
                                         sem.at[slot])

        for j in range(min(STREAM_DEPTH - 1, n_chunks)):
            copy(j, j).start()

        def body(i, carry):
            ahead = i + (STREAM_DEPTH - 1)

            @pl.when(ahead < n_chunks)
            def _():
                copy(ahead, lax.rem(ahead, STREAM_DEPTH)).start()

            slot = lax.rem(i, STREAM_DEPTH)
            copy(i, slot).wait()
            dst[pl.ds(pl.multiple_of(i * rows, rows), rows), :] = stage[slot].astype(BF16)
            return carry

        lax.fori_loop(0, n_chunks, body, 0)

    pl.run_scoped(scoped, pltpu.VMEM((STREAM_DEPTH, rows, src.shape[1]), F32),
                  pltpu.SemaphoreType.DMA((STREAM_DEPTH,)))


def _first_step():
    return (pl.program_id(0) == 0) & (pl.program_id(1) == 0)


def _layer_rows(layer, *refs):
    return [r.at[pl.ds(layer, 1), :] for r in refs]


def _mixer_kernel(x_ref, g_all, lng_all, lnb_all, cb_all, clg_all, clb_all, ps_all,
                  win_hbm, ws_ref, bs_ref, cw_ref, sw_ref, pw_ref,
                  wbr_hbm, wout_hbm, o_ref,
                  win_ref, wbr_ref, wout_ref,
                  h_buf, y_buf, gate_buf, conf_buf, short_buf, pool_buf, *, layer):
    s = pl.program_id(1)
    tm = x_ref.shape[0]
    g_ref, lng_ref, lnb_ref, cb_ref, clg_ref, clb_ref, ps_ref = _layer_rows(
        layer, g_all, lng_all, lnb_all, cb_all, clg_all, clb_all, ps_all)

    @pl.when(_first_step())
    def _():
        _stream_to_bf16(win_hbm.at[layer], win_ref)
        _stream_to_bf16(wbr_hbm.at[layer], wbr_ref)
        _stream_to_bf16(wout_hbm.at[layer], wout_ref)

    @pl.when(s == 0)
    def _():
        conf_buf[:, 0:CONF_HALO, :] = jnp.zeros((MIX_TILES, CONF_HALO, LANES), F32)
        short_buf[:, 0:SHORT_HALO, :] = jnp.zeros((MIX_TILES, SHORT_HALO, LANES), F32)
        pool_buf[:, 0:POOL_HALO, :] = jnp.zeros((MIX_TILES, POOL_HALO, LANES), F32)

    def store_tiles(buf, halo, value):
        for c in range(MIX_TILES):
            buf[c, halo:halo + tm, :] = value[:, c * LANES:(c + 1) * LANES]

    h_buf[...] = _rms_norm(x_ref[...], g_ref[...]).astype(BF16)

    z = _dot(h_buf[...], win_ref[:, 0:G_OFF])
    zb = z[:, B_OFF:B_OFF + 2 * BW]
    store_tiles(conf_buf, CONF_HALO, zb[:, :BW] * jax.nn.sigmoid(zb[:, BW:]))
    zc = z[:, C_OFF:C_OFF + 3 * BW]
    store_tiles(short_buf, SHORT_HALO, zc[:, BW:2 * BW] * zc[:, 2 * BW:])
    zd = z[:, D_OFF:D_OFF + BW]
    store_tiles(pool_buf, POOL_HALO, zd)
    za = jax.nn.gelu(z[:, A_OFF:A_OFF + 2 * BW])

    def merge_gate(k):
        zg = _dot(h_buf[...], win_ref[:, G_OFF + k * D_MODEL:G_OFF + (k + 1) * D_MODEL])
        gate_buf[k] = jax.nn.sigmoid(zg)

    merge_gate(0)
    merge_gate(1)

    def causal_taps(buf, halo, w_ref):
        n_taps = w_ref.shape[0]
        base = halo - (n_taps - 1)
        outs = []
        for c in range(MIX_TILES):
            lanes = slice(c * LANES, (c + 1) * LANES)
            acc = buf[c, base:base + tm, :] * w_ref[0:1, lanes]
            for k in range(1, n_taps):
                acc = acc + buf[c, base + k:base + k + tm, :] * w_ref[k:k + 1, lanes]
            buf[c, 0:halo, :] = buf[c, tm:tm + halo, :]
            outs.append(acc)
        return jnp.concatenate(outs, axis=1)

    yb = causal_taps(conf_buf, CONF_HALO, cw_ref) + cb_ref[...]

    y_buf[2] = (zc[:, :BW] * causal_taps(short_buf, SHORT_HALO, sw_ref)).astype(BF16)

    t_pos = s * tm + lax.broadcasted_iota(jnp.int32, (tm, LANES), 0) + 1
    low = lax.broadcasted_iota(jnp.int32, (tm, LANES), 1) < GROUP_DIM
    span = POOL_HALO - SUBLANES + tm
    pooled = []
    for c in range(MIX_TILES):
        w_lo, w_hi = POOL_WINDOWS[2 * c], POOL_WINDOWS[2 * c + 1]
        carry_rows = pool_buf[c, tm:tm + POOL_HALO, :]
        w = 1
        while w < w_hi:
            doubled = (pool_buf[c, SUBLANES:SUBLANES + span, :]
                       + pool_buf[c, SUBLANES - w:SUBLANES - w + span, :])
            pool_buf[c, SUBLANES:SUBLANES + span, :] = doubled
            w *= 2
            if w == w_lo:
                lo_sum = doubled[POOL_HALO - SUBLANES:, :]
        win_sum = jnp.where(low, lo_sum, doubled[POOL_HALO - SUBLANES:, :])
        cnt = jnp.minimum(t_pos, jnp.where(low, w_lo, w_hi)).astype(F32)
        pooled.append(win_sum / cnt - zd[:, c * LANES:(c + 1) * LANES])
        pool_buf[c, 0:POOL_HALO, :] = carry_rows
    pooled = jnp.concatenate(pooled, axis=1).astype(BF16)

    y_buf[3] = (_dot(pooled, pw_ref[...]) * ps_ref[...]).astype(BF16)
    merge_gate(2)
    gi = lax.broadcasted_iota(jnp.int32, (BW, BW), 0) >> GROUP_SHIFT
    gj = lax.broadcasted_iota(jnp.int32, (BW, BW), 1) >> GROUP_SHIFT
    g_avg = jnp.where(gi == gj, 1.0 / GROUP_DIM, 0.0).astype(BF16)
    ybc = yb - _dot(yb.astype(BF16), g_avg)
    merge_gate(3)

    u = za[:, :BW]
    v = za[:, BW:]
    mu = jnp.mean(v, axis=-1, keepdims=True)
    vc = v - mu
    var = jnp.mean(vc * vc, axis=-1, keepdims=True)
    v = (vc * lax.rsqrt(var + EPS) * lng_ref[...] + lnb_ref[...]).astype(BF16)
    row = lax.broadcasted_iota(jnp.int32, (CHUNK, N_GROUPS * CHUNK), 0)
    col = lax.broadcasted_iota(jnp.int32, (CHUNK, N_GROUPS * CHUNK), 1)
    w_all = jnp.concatenate([ws_ref[g] for g in range(N_GROUPS)], axis=1).astype(BF16)
    w_cat = jnp.where((col & (CHUNK - 1)) <= row, w_all, jnp.zeros((), BF16))
    lane_group = lax.broadcasted_iota(jnp.int32, (CHUNK, BW), 1) >> GROUP_SHIFT
    for n in range(tm // CHUNK):
        vn = v[n * CHUNK:(n + 1) * CHUNK, :]
        v_bd = jnp.concatenate(
            [jnp.where(lane_group == g, vn, jnp.zeros((), BF16)) for g in range(N_GROUPS)],
            axis=0)
        mixed = _dot(w_cat, v_bd) + bs_ref[...]
        y_buf[0, n * CHUNK:(n + 1) * CHUNK, :] = (
            u[n * CHUNK:(n + 1) * CHUNK, :] * mixed).astype(BF16)
    gvar = _dot((ybc * ybc).astype(BF16), g_avg)
    ybn = ybc * lax.rsqrt(gvar + EPS) * clg_ref[...] + clb_ref[...]
    y_buf[1] = (ybn * jax.nn.sigmoid(ybn)).astype(BF16)

    merged = None
    for k in (2, 3, 0, 1):
        term = _dot(y_buf[k], wbr_ref[k * BW:(k + 1) * BW, :]) * gate_buf[k]
        merged = term if merged is None else merged + term
    o_ref[...] = x_ref[...] + _dot(merged.astype(BF16), wout_ref[...])


def _ffn_kernel(x_ref, p_ref, gffn_all, bconv_all, gple_all, gfin_ref,
                wup_hbm, wconv_ref, wdown_hbm, wpg_hbm, wpp_hbm, o_ref,
                wup_ref, wdown_ref, wpg_ref, wpp_ref,
                h_buf, act_buf, acc_buf, up_buf, *, layer, final_norm):
    s = pl.program_id(1)
    tm = x_ref.shape[0]
    gffn_ref, bconv_ref, gple_ref = _layer_rows(layer, gffn_all, bconv_all, gple_all)

    @pl.when(_first_step())
    def _():
        _stream_to_bf16(wup_hbm.at[layer], wup_ref)
        _stream_to_bf16(wdown_hbm.at[layer], wdown_ref)
        _stream_to_bf16(wpg_hbm.at[layer], wpg_ref)
        _stream_to_bf16(wpp_hbm.at[layer], wpp_ref)

    @pl.when(s == 0)
    def _():
        up_buf[:, 0:SHORT_HALO, :] = jnp.zeros((2 * D_FF // LANES, SHORT_HALO, LANES), F32)

    x = x_ref[...]
    h_buf[...] = _rms_norm(x, gffn_ref[...]).astype(BF16)

    base = SHORT_HALO - (FFN_K - 1)

    def up_chunk(j):
        parts = []
        for part in range(2):
            c0 = part * D_FF + j * FFN_COLS
            up = _dot(h_buf[...], wup_ref[:, c0:c0 + FFN_COLS])
            convs = []
            for c in range(FFN_COLS // LANES):
                tile = c0 // LANES + c
                lanes = slice(tile * LANES, (tile + 1) * LANES)
                up_buf[tile, SHORT_HALO:SHORT_HALO + tm, :] = up[:, c * LANES:(c + 1) * LANES]
                conv = up_buf[tile, base:base + tm, :] * wconv_ref[0:1, lanes]
                for k in range(1, FFN_K):
                    conv = conv + up_buf[tile, base + k:base + k + tm, :] * wconv_ref[k:k + 1, lanes]
                up_buf[tile, 0:SHORT_HALO, :] = up_buf[tile, tm:tm + SHORT_HALO, :]
                convs.append(conv)
            parts.append(jnp.concatenate(convs, axis=1) + bconv_ref[:, c0:c0 + FFN_COLS])
        gate, val = parts
        act_buf[:, j * FFN_COLS:(j + 1) * FFN_COLS] = (
            gate * jax.nn.sigmoid(gate) * val).astype(BF16)

    def down_group(first, last):
        rows = slice(first * FFN_COLS, last * FFN_COLS)
        partial = _dot(act_buf[:, rows], wdown_ref[rows, :])
        acc_buf[...] = (x if first == 0 else acc_buf[...]) + partial

    n_chunks = D_FF // FFN_COLS
    for j in range(n_chunks):
        up_chunk(j)
    for first in range(0, n_chunks, DOWN_GROUP):
        down_group(first, min(first + DOWN_GROUP, n_chunks))

    proj = _dot(p_ref[...].astype(BF16), wpp_ref[...])
    block = tm // PLE_ROW_BLOCKS
    for r in range(PLE_ROW_BLOCKS):
        rows = slice(r * block, (r + 1) * block)
        y = acc_buf[rows, :]
        h2 = _rms_norm(y, gple_ref[...]).astype(BF16)
        gate = jax.nn.sigmoid(_dot(h2, wpg_ref[...]))
        y = y + gate * proj[rows, :]
        if final_norm:
            y = _rms_norm(y, gfin_ref[...])
        o_ref[rows, :] = y


def _const_spec(shape, layer):
    nd = len(shape)
    return pl.BlockSpec((None,) + tuple(shape), lambda b, s: (layer,) + (0,) * nd,
                        pipeline_mode=pl.Buffered(1))


def _tile_spec(tm, width):
    return pl.BlockSpec((None, tm, width), lambda b, s: (b, s, 0))


def _compiler_params():
    return pltpu.CompilerParams(dimension_semantics=("arbitrary", "arbitrary"),
                                vmem_limit_bytes=VMEM_LIMIT_BYTES)


class _Streamed(NamedTuple):
    array: jax.Array


class _Whole(NamedTuple):
    array: jax.Array


def _param_specs(params, layer):
    specs, weight_bufs, operands = [], [], []
    for q in params:
        if isinstance(q, _Streamed):
            specs.append(pl.BlockSpec(memory_space=pl.ANY))
            weight_bufs.append(pltpu.VMEM(q.array.shape[1:], BF16))
            operands.append(q.array)
        elif isinstance(q, _Whole):
            specs.append(pl.BlockSpec(q.array.shape, lambda b, s, nd=q.array.ndim: (0,) * nd,
                                      pipeline_mode=pl.Buffered(1)))
            operands.append(q.array)
        else:
            specs.append(_const_spec(q.shape[1:], layer))
            operands.append(q)
    return specs, weight_bufs, operands


def _mixer_call(x, layer, params):
    batch, seq, _ = x.shape
    param_specs, weight_bufs, operands = _param_specs(params, layer)
    return pl.pallas_call(
        functools.partial(_mixer_kernel, layer=layer),
        grid=(batch, seq // TM_MIX),
        in_specs=[_tile_spec(TM_MIX, D_MODEL)] + param_specs,
        out_specs=_tile_spec(TM_MIX, D_MODEL),
        out_shape=jax.ShapeDtypeStruct(x.shape, F32),
        scratch_shapes=weight_bufs + [
            pltpu.VMEM((TM_MIX, D_MODEL), BF16),
            pltpu.VMEM((N_BRANCH, TM_MIX, BW), BF16),
            pltpu.VMEM((N_BRANCH, TM_MIX, D_MODEL), F32),
            pltpu.VMEM((MIX_TILES, CONF_HALO + TM_MIX, LANES), F32),
            pltpu.VMEM((MIX_TILES, SHORT_HALO + TM_MIX, LANES), F32),
            pltpu.VMEM((MIX_TILES, POOL_HALO + TM_MIX, LANES), F32),
        ],
        compiler_params=_compiler_params(),
        name=f"mixer_l{layer}",
    )(x, *operands)


def _ffn_call(x, p, layer, params, final_norm):
    batch, seq, _ = x.shape
    param_specs, weight_bufs, operands = _param_specs(params, layer)
    in_specs = ([_tile_spec(TM_FFN, D_MODEL),
                 pl.BlockSpec((None, None, TM_FFN, PLE_DIM), lambda b, s: (layer, b, s, 0))]
                + param_specs)
    return pl.pallas_call(
        functools.partial(_ffn_kernel, layer=layer, final_norm=final_norm),
        grid=(batch, seq // TM_FFN),
        in_specs=in_specs,
        out_specs=_tile_spec(TM_FFN, D_MODEL),
        out_shape=jax.ShapeDtypeStruct(x.shape, F32),
        scratch_shapes=weight_bufs + [
            pltpu.VMEM((TM_FFN, D_MODEL), BF16),
            pltpu.VMEM((TM_FFN, D_FF), BF16),
            pltpu.VMEM((TM_FFN, D_MODEL), F32),
            pltpu.VMEM((2 * D_FF // LANES, SHORT_HALO + TM_FFN, LANES), F32),
        ],
        compiler_params=_compiler_params(),
        name=f"ffn_l{layer}",
    )(x, p, *operands)


def kernel(x, p, g_mix, w_in, gmlp_ln_g, gmlp_ln_b, gmlp_w_s, gmlp_b_s, conf_w_dw, conf_b_dw, conf_ln_g, conf_ln_b, short_w, pool_w, pool_scale, w_branch, w_out, g_ffn, ffn_w_up, ffn_w_conv, ffn_b_conv, ffn_w_down, g_ple, ple_w_gate, ple_w_proj, g_final):
    depth = w_in.shape[0]
    assert x.shape[1] % TM_MIX == 0 and TM_MIX % CHUNK == 0 and x.shape[1] % TM_FFN == 0
    flat = lambda a: a.reshape(depth, -1)

    pool_bd = jnp.einsum('gh,lgcd->lgchd', jnp.eye(N_GROUPS, dtype=F32), pool_w)
    mixer_params = [
        _Whole(g_mix), _Whole(gmlp_ln_g), _Whole(gmlp_ln_b), _Whole(conf_b_dw),
        _Whole(flat(conf_ln_g)), _Whole(flat(conf_ln_b)), _Whole(flat(pool_scale)),
        _Streamed(w_in),
        gmlp_w_s,
        jnp.repeat(gmlp_b_s.transpose(0, 2, 1), GROUP_DIM, axis=2),
        conf_w_dw,
        short_w,
        pool_bd.reshape(depth, BW, BW).astype(BF16),
        _Streamed(w_branch.reshape(depth, N_BRANCH * BW, D_MODEL)), _Streamed(w_out),
    ]
    ffn_params = [
        _Whole(g_ffn), _Whole(ffn_b_conv), _Whole(g_ple), _Whole(g_final.reshape(1, D_MODEL)),
        _Streamed(ffn_w_up), ffn_w_conv,
        _Streamed(ffn_w_down), _Streamed(ple_w_gate), _Streamed(ple_w_proj),
    ]

    for i in range(depth):
        x = _mixer_call(x, i, mixer_params)
        x = _ffn_call(x, p, i, ffn_params, final_norm=(i == depth - 1))
    return x
```

```python
import functools
from typing import NamedTuple

import jax
import jax.numpy as jnp
from jax import lax
from jax.experimental import pallas as pl
from jax.experimental.pallas import tpu as pltpu

D_MODEL = 1024
PLE_DIM = 256
N_BRANCH = 4
BW = 256
N_GROUPS = 4
GROUP_DIM = BW // N_GROUPS
GROUP_SHIFT = GROUP_DIM.bit_length() - 1
CHUNK = 128
CONF_K = 31
SHORT_K = 3
FFN_K = 3
D_FF = 2816
POOL_WINDOWS = (2, 4, 8, 16)
EPS = 1e-6

A_OFF, B_OFF, C_OFF, D_OFF, G_OFF = 0, 2 * BW, 4 * BW, 7 * BW, 8 * BW
IN_COLS = G_OFF + N_BRANCH * D_MODEL

SUBLANES = 8
LANES = 128
MIX_TILES = BW // LANES
TM_MIX = 512
TM_FFN = 512
CONF_HALO = 32
SHORT_HALO = 8
POOL_HALO = 24
assert POOL_HALO - SUBLANES >= max(POOL_WINDOWS) - 2
assert LANES == 2 * GROUP_DIM and len(POOL_WINDOWS) == 2 * MIX_TILES
FFN_COLS = 256
DOWN_GROUP = 2
PLE_ROW_BLOCKS = 4
BF16_SUBLANES = 16
STREAM_CHUNK_BYTES = 3 * 1024 * 1024
STREAM_DEPTH = 3
VMEM_LIMIT_BYTES = 56 * 1024 * 1024

BF16 = jnp.bfloat16
F32 = jnp.float32


def _dot(a, b):
    return jnp.dot(a, b, preferred_element_type=F32)


def _rms_norm(x, g):
    ms = jnp.mean(x * x, axis=-1, keepdims=True)
    return x * lax.rsqrt(ms + EPS) * g


def _stream_rows(n_rows, n_cols):
    fits = [r for r in range(BF16_SUBLANES, n_rows + 1, BF16_SUBLANES)
            if n_rows % r == 0 and r * n_cols * 4 <= STREAM_CHUNK_BYTES]
    return fits[-1]


def _stream_to_bf16(src, dst):
    rows = _stream_rows(*src.shape)
    n_chunks = src.shape[0] // rows

    def scoped(stage, sem):
        def copy(i, slot):
            return pltpu.make_async_copy(src.at[pl.ds(i * rows, rows), :], stage.at[slot],
                                         sem.at[slot])

        for j in range(min(STREAM_DEPTH - 1, n_chunks)):
            copy(j, j).start()

        def body(i, carry):
            ahead = i + (STREAM_DEPTH - 1)

            @pl.when(ahead < n_chunks)
            def _():
                copy(ahead, lax.rem(ahead, STREAM_DEPTH)).start()

            slot = lax.rem(i, STREAM_DEPTH)
            copy(i, slot).wait()
            dst[pl.ds(pl.multiple_of(i * rows, rows), rows), :] = stage[slot].astype(BF16)
            return carry

        lax.fori_loop(0, n_chunks, body, 0)

    pl.run_scoped(scoped, pltpu.VMEM((STREAM_DEPTH, rows, src.shape[1]), F32),
                  pltpu.SemaphoreType.DMA((STREAM_DEPTH,)))


def _first_step():
    return (pl.program_id(0) == 0) & (pl.program_id(1) == 0)


def _layer_rows(layer, *refs):
    return [r.at[pl.ds(layer, 1), :] for r in refs]


def _mixer_kernel(x_ref, g_all, lng_all, lnb_all, cb_all, clg_all, clb_all, ps_all,
                  win_hbm, ws_ref, bs_ref, cw_ref, sw_ref, pw_ref,
                  wbr_hbm, wout_hbm, o_ref,
                  win_ref, wbr_ref, wout_ref,
                  h_buf, y_buf, gate_buf, conf_buf, short_buf, pool_buf, *, layer):
    s = pl.program_id(1)
    tm = x_ref.shape[0]
    g_ref, lng_ref, lnb_ref, cb_ref, clg_ref, clb_ref, ps_ref = _layer_rows(
        layer, g_all, lng_all, lnb_all, cb_all, clg_all, clb_all, ps_all)

    @pl.when(_first_step())
    def _():
        _stream_to_bf16(win_hbm.at[layer], win_ref)
        _stream_to_bf16(wbr_hbm.at[layer], wbr_ref)
        _stream_to_bf16(wout_hbm.at[layer], wout_ref)

    @pl.when(s == 0)
    def _():
        conf_buf[:, 0:CONF_HALO, :] = jnp.zeros((MIX_TILES, CONF_HALO, LANES), F32)
        short_buf[:, 0:SHORT_HALO, :] = jnp.zeros((MIX_TILES, SHORT_HALO, LANES), F32)
        pool_buf[:, 0:POOL_HALO, :] = jnp.zeros((MIX_TILES, POOL_HALO, LANES), F32)

    def store_tiles(buf, halo, value):
        for c in range(MIX_TILES):
            buf[c, halo:halo + tm, :] = value[:, c * LANES:(c + 1) * LANES]

    h_buf[...] = _rms_norm(x_ref[...], g_ref[...]).astype(BF16)

    z = _dot(h_buf[...], win_ref[:, 0:G_OFF])
    zb = z[:, B_OFF:B_OFF + 2 * BW]
    store_tiles(conf_buf, CONF_HALO, zb[:, :BW] * jax.nn.sigmoid(zb[:, BW:]))
    zc = z[:, C_OFF:C_OFF + 3 * BW]
    store_tiles(short_buf, SHORT_HALO, zc[:, BW:2 * BW] * zc[:, 2 * BW:])
    zd = z[:, D_OFF:D_OFF + BW]
    store_tiles(pool_buf, POOL_HALO, zd)
    za = jax.nn.gelu(z[:, A_OFF:A_OFF + 2 * BW])

    def merge_gate(k):
        zg = _dot(h_buf[...], win_ref[:, G_OFF + k * D_MODEL:G_OFF + (k + 1) * D_MODEL])
        gate_buf[k] = jax.nn.sigmoid(zg)

    merge_gate(0)
    merge_gate(1)

    def causal_taps(buf, halo, w_ref):
        n_taps = w_ref.shape[0]
        base = halo - (n_taps - 1)
        outs = []
        for c in range(MIX_TILES):
            lanes = slice(c * LANES, (c + 1) * LANES)
            acc = buf[c, base:base + tm, :] * w_ref[0:1, lanes]
            for k in range(1, n_taps):
                acc = acc + buf[c, base + k:base + k + tm, :] * w_ref[k:k + 1, lanes]
            buf[c, 0:halo, :] = buf[c, tm:tm + halo, :]
            outs.append(acc)
        return jnp.concatenate(outs, axis=1)

    yb = causal_taps(conf_buf, CONF_HALO, cw_ref) + cb_ref[...]

    y_buf[2] = (zc[:, :BW] * causal_taps(short_buf, SHORT_HALO, sw_ref)).astype(BF16)

    t_pos = s * tm + lax.broadcasted_iota(jnp.int32, (tm, LANES), 0) + 1
    low = lax.broadcasted_iota(jnp.int32, (tm, LANES), 1) < GROUP_DIM
    span = POOL_HALO - SUBLANES + tm
    pooled = []
    for c in range(MIX_TILES):
        w_lo, w_hi = POOL_WINDOWS[2 * c], POOL_WINDOWS[2 * c + 1]
        carry_rows = pool_buf[c, tm:tm + POOL_HALO, :]
        w = 1
        while w < w_hi:
            doubled = (pool_buf[c, SUBLANES:SUBLANES + span, :]
                       + pool_buf[c, SUBLANES - w:SUBLANES - w + span, :])
            pool_buf[c, SUBLANES:SUBLANES + span, :] = doubled
            w *= 2
            if w == w_lo:
                lo_sum = doubled[POOL_HALO - SUBLANES:, :]
        win_sum = jnp.where(low, lo_sum, doubled[POOL_HALO - SUBLANES:, :])
        cnt = jnp.minimum(t_pos, jnp.where(low, w_lo, w_hi)).astype(F32)
        pooled.append(win_sum / cnt - zd[:, c * LANES:(c + 1) * LANES])
        pool_buf[c, 0:POOL_HALO, :] = carry_rows
    pooled = jnp.concatenate(pooled, axis=1).astype(BF16)

    y_buf[3] = (_dot(pooled, pw_ref[...]) * ps_ref[...]).astype(BF16)
    merge_gate(2)
    def group_mean(v):
        outs = []
        for c in range(MIX_TILES):
            vc = v[:, c * LANES:(c + 1) * LANES]
            s_all = jnp.sum(vc, axis=-1, keepdims=True)
            s_lo = jnp.sum(jnp.where(low, vc, 0.0), axis=-1, keepdims=True)
            outs.append(jnp.where(low, s_lo, s_all - s_lo) * (1.0 / GROUP_DIM))
        return jnp.concatenate(outs, axis=1)

    ybc = yb - group_mean(yb)
    merge_gate(3)

    u = za[:, :BW]
    v = za[:, BW:]
    mu = jnp.mean(v, axis=-1, keepdims=True)
    vc = v - mu
    var = jnp.mean(vc * vc, axis=-1, keepdims=True)
    v = (vc * lax.rsqrt(var + EPS) * lng_ref[...] + lnb_ref[...]).astype(BF16)
    row = lax.broadcasted_iota(jnp.int32, (CHUNK, N_GROUPS * CHUNK), 0)
    col = lax.broadcasted_iota(jnp.int32, (CHUNK, N_GROUPS * CHUNK), 1)
    w_all = jnp.concatenate([ws_ref[g] for g in range(N_GROUPS)], axis=1).astype(BF16)
    w_cat = jnp.where((col & (CHUNK - 1)) <= row, w_all, jnp.zeros((), BF16))
    lane_group = lax.broadcasted_iota(jnp.int32, (CHUNK, BW), 1) >> GROUP_SHIFT
    for n in range(tm // CHUNK):
        vn = v[n * CHUNK:(n + 1) * CHUNK, :]
        v_bd = jnp.concatenate(
            [jnp.where(lane_group == g, vn, jnp.zeros((), BF16)) for g in range(N_GROUPS)],
            axis=0)
        mixed = _dot(w_cat, v_bd) + bs_ref[...]
        y_buf[0, n * CHUNK:(n + 1) * CHUNK, :] = (
            u[n * CHUNK:(n + 1) * CHUNK, :] * mixed).astype(BF16)
    gvar = group_mean(ybc * ybc)
    ybn = ybc * lax.rsqrt(gvar + EPS) * clg_ref[...] + clb_ref[...]
    y_buf[1] = (ybn * jax.nn.sigmoid(ybn)).astype(BF16)

    merged = None
    for k in (2, 3, 0, 1):
        term = _dot(y_buf[k], wbr_ref[k * BW:(k + 1) * BW, :]) * gate_buf[k]
        merged = term if merged is None else merged + term
    o_ref[...] = x_ref[...] + _dot(merged.astype(BF16), wout_ref[...])


def _ffn_kernel(x_ref, p_ref, gffn_all, bconv_all, gple_all, gfin_ref,
                wup_hbm, wconv_ref, wdown_hbm, wpg_hbm, wpp_hbm, o_ref,
                wup_ref, wdown_ref, wpg_ref, wpp_ref,
                h_buf, act_buf, acc_buf, up_buf, *, layer, final_norm):
    s = pl.program_id(1)
    tm = x_ref.shape[0]
    gffn_ref, bconv_ref, gple_ref = _layer_rows(layer, gffn_all, bconv_all, gple_all)

    @pl.when(_first_step())
    def _():
        _stream_to_bf16(wup_hbm.at[layer], wup_ref)
        _stream_to_bf16(wdown_hbm.at[layer], wdown_ref)
        _stream_to_bf16(wpg_hbm.at[layer], wpg_ref)
        _stream_to_bf16(wpp_hbm.at[layer], wpp_ref)

    @pl.when(s == 0)
    def _():
        up_buf[:, 0:SHORT_HALO, :] = jnp.zeros((2 * D_FF // LANES, SHORT_HALO, LANES), F32)

    x = x_ref[...]
    h_buf[...] = _rms_norm(x, gffn_ref[...]).astype(BF16)

    base = SHORT_HALO - (FFN_K - 1)

    def up_chunk(j):
        parts = []
        for part in range(2):
            c0 = part * D_FF + j * FFN_COLS
            up = _dot(h_buf[...], wup_ref[:, c0:c0 + FFN_COLS])
            convs = []
            for c in range(FFN_COLS // LANES):
                tile = c0 // LANES + c
                lanes = slice(tile * LANES, (tile + 1) * LANES)
                up_buf[tile, SHORT_HALO:SHORT_HALO + tm, :] = up[:, c * LANES:(c + 1) * LANES]
                conv = up_buf[tile, base:base + tm, :] * wconv_ref[0:1, lanes]
                for k in range(1, FFN_K):
                    conv = conv + up_buf[tile, base + k:base + k + tm, :] * wconv_ref[k:k + 1, lanes]
                up_buf[tile, 0:SHORT_HALO, :] = up_buf[tile, tm:tm + SHORT_HALO, :]
                convs.append(conv)
            parts.append(jnp.concatenate(convs, axis=1) + bconv_ref[:, c0:c0 + FFN_COLS])
        gate, val = parts
        act_buf[:, j * FFN_COLS:(j + 1) * FFN_COLS] = (
            gate * jax.nn.sigmoid(gate) * val).astype(BF16)

    def down_group(first, last):
        rows = slice(first * FFN_COLS, last * FFN_COLS)
        partial = _dot(act_buf[:, rows], wdown_ref[rows, :])
        acc_buf[...] = (x if first == 0 else acc_buf[...]) + partial

    n_chunks = D_FF // FFN_COLS
    for j in range(n_chunks):
        up_chunk(j)
    for first in range(0, n_chunks, DOWN_GROUP):
        down_group(first, min(first + DOWN_GROUP, n_chunks))

    proj = _dot(p_ref[...].astype(BF16), wpp_ref[...])
    block = tm // PLE_ROW_BLOCKS
    for r in range(PLE_ROW_BLOCKS):
        rows = slice(r * block, (r + 1) * block)
        y = acc_buf[rows, :]
        h2 = _rms_norm(y, gple_ref[...]).astype(BF16)
        gate = jax.nn.sigmoid(_dot(h2, wpg_ref[...]))
        y = y + gate * proj[rows, :]
        if final_norm:
            y = _rms_norm(y, gfin_ref[...])
        o_ref[rows, :] = y


def _const_spec(shape, layer):
    nd = len(shape)
    return pl.BlockSpec((None,) + tuple(shape), lambda b, s: (layer,) + (0,) * nd,
                        pipeline_mode=pl.Buffered(1))


def _tile_spec(tm, width):
    return pl.BlockSpec((None, tm, width), lambda b, s: (b, s, 0))


def _compiler_params():
    return pltpu.CompilerParams(dimension_semantics=("arbitrary", "arbitrary"),
                                vmem_limit_bytes=VMEM_LIMIT_BYTES)


class _Streamed(NamedTuple):
    array: jax.Array


class _Whole(NamedTuple):
    array: jax.Array


def _param_specs(params, layer):
    specs, weight_bufs, operands = [], [], []
    for q in params:
        if isinstance(q, _Streamed):
            specs.append(pl.BlockSpec(memory_space=pl.ANY))
            weight_bufs.append(pltpu.VMEM(q.array.shape[1:], BF16))
            operands.append(q.array)
        elif isinstance(q, _Whole):
            specs.append(pl.BlockSpec(q.array.shape, lambda b, s, nd=q.array.ndim: (0,) * nd,
                                      pipeline_mode=pl.Buffered(1)))
            operands.append(q.array)
        else:
            specs.append(_const_spec(q.shape[1:], layer))
            operands.append(q)
    return specs, weight_bufs, operands


def _mixer_call(x, layer, params):
    batch, seq, _ = x.shape
    param_specs, weight_bufs, operands = _param_specs(params, layer)
    return pl.pallas_call(
        functools.partial(_mixer_kernel, layer=layer),
        grid=(batch, seq // TM_MIX),
        in_specs=[_tile_spec(TM_MIX, D_MODEL)] + param_specs,
        out_specs=_tile_spec(TM_MIX, D_MODEL),
        out_shape=jax.ShapeDtypeStruct(x.shape, F32),
        scratch_shapes=weight_bufs + [
            pltpu.VMEM((TM_MIX, D_MODEL), BF16),
            pltpu.VMEM((N_BRANCH, TM_MIX, BW), BF16),
            pltpu.VMEM((N_BRANCH, TM_MIX, D_MODEL), F32),
            pltpu.VMEM((MIX_TILES, CONF_HALO + TM_MIX, LANES), F32),
            pltpu.VMEM((MIX_TILES, SHORT_HALO + TM_MIX, LANES), F32),
            pltpu.VMEM((MIX_TILES, POOL_HALO + TM_MIX, LANES), F32),
        ],
        compiler_params=_compiler_params(),
        name=f"mixer_l{layer}",
    )(x, *operands)


def _ffn_call(x, p, layer, params, final_norm):
    batch, seq, _ = x.shape
    param_specs, weight_bufs, operands = _param_specs(params, layer)
    in_specs = ([_tile_spec(TM_FFN, D_MODEL),
                 pl.BlockSpec((None, None, TM_FFN, PLE_DIM), lambda b, s: (layer, b, s, 0))]
                + param_specs)
    return pl.pallas_call(
        functools.partial(_ffn_kernel, layer=layer, final_norm=final_norm),
        grid=(batch, seq // TM_FFN),
        in_specs=in_specs,
        out_specs=_tile_spec(TM_FFN, D_MODEL),
        out_shape=jax.ShapeDtypeStruct(x.shape, F32),
        scratch_shapes=weight_bufs + [
            pltpu.VMEM((TM_FFN, D_MODEL), BF16),
            pltpu.VMEM((TM_FFN, D_FF), BF16),
            pltpu.VMEM((TM_FFN, D_MODEL), F32),
            pltpu.VMEM((2 * D_FF // LANES, SHORT_HALO + TM_FFN, LANES), F32),
        ],
        compiler_params=_compiler_params(),
        name=f"ffn_l{layer}",
    )(x, p, *operands)


def kernel(x, p, g_mix, w_in, gmlp_ln_g, gmlp_ln_b, gmlp_w_s, gmlp_b_s, conf_w_dw, conf_b_dw, conf_ln_g, conf_ln_b, short_w, pool_w, pool_scale, w_branch, w_out, g_ffn, ffn_w_up, ffn_w_conv, ffn_b_conv, ffn_w_down, g_ple, ple_w_gate, ple_w_proj, g_final):
    depth = w_in.shape[0]
    assert x.shape[1] % TM_MIX == 0 and TM_MIX % CHUNK == 0 and x.shape[1] % TM_FFN == 0
    flat = lambda a: a.reshape(depth, -1)

    pool_bd = jnp.einsum('gh,lgcd->lgchd', jnp.eye(N_GROUPS, dtype=F32), pool_w)
    mixer_params = [
        _Whole(g_mix), _Whole(gmlp_ln_g), _Whole(gmlp_ln_b), _Whole(conf_b_dw),
        _Whole(flat(conf_ln_g)), _Whole(flat(conf_ln_b)), _Whole(flat(pool_scale)),
        _Streamed(w_in),
        gmlp_w_s,
        jnp.repeat(gmlp_b_s.transpose(0, 2, 1), GROUP_DIM, axis=2),
        conf_w_dw,
        short_w,
        pool_bd.reshape(depth, BW, BW).astype(BF16),
        _Streamed(w_branch.reshape(depth, N_BRANCH * BW, D_MODEL)), _Streamed(w_out),
    ]
    ffn_params = [
        _Whole(g_ffn), _Whole(ffn_b_conv), _Whole(g_ple), _Whole(g_final.reshape(1, D_MODEL)),
        _Streamed(ffn_w_up), ffn_w_conv,
        _Streamed(ffn_w_down), _Streamed(ple_w_gate), _Streamed(ple_w_proj),
    ]

    for i in range(depth):
        x = _mixer_call(x, i, mixer_params)
        x = _ffn_call(x, p, i, ffn_params, final_norm=(i == depth - 1))
    return x
```

```python
import functools
from typing import NamedTuple

import jax
import jax.numpy as jnp
from jax import lax
from jax.experimental import pallas as pl
from jax.experimental.pallas import tpu as pltpu

D_MODEL = 1024
PLE_DIM = 256
N_BRANCH = 4
BW = 256
N_GROUPS = 4
GROUP_DIM = BW // N_GROUPS
GROUP_SHIFT = GROUP_DIM.bit_length() - 1
CHUNK = 128
CONF_K = 31
SHORT_K = 3
FFN_K = 3
D_FF = 2816
POOL_WINDOWS = (2, 4, 8, 16)
EPS = 1e-6

A_OFF, B_OFF, C_OFF, D_OFF, G_OFF = 0, 2 * BW, 4 * BW, 7 * BW, 8 * BW
IN_COLS = G_OFF + N_BRANCH * D_MODEL

SUBLANES = 8
LANES = 128
MIX_TILES = BW // LANES
TM_MIX = 512
TM_FFN = 512
CONF_HALO = 32
SHORT_HALO = 8
POOL_HALO = 24
assert POOL_HALO - SUBLANES >= max(POOL_WINDOWS) - 2
assert LANES == 2 * GROUP_DIM and len(POOL_WINDOWS) == 2 * MIX_TILES
FFN_COLS = 256
DOWN_GROUP = 2
PLE_ROW_BLOCKS = 4
BF16_SUBLANES = 16
STREAM_CHUNK_BYTES = 3 * 1024 * 1024
STREAM_DEPTH = 3
VMEM_LIMIT_BYTES = 56 * 1024 * 1024

BF16 = jnp.bfloat16
F32 = jnp.float32


def _dot(a, b):
    return jnp.dot(a, b, preferred_element_type=F32)


def _rms_norm(x, g):
    ms = jnp.mean(x * x, axis=-1, keepdims=True)
    return x * lax.rsqrt(ms + EPS) * g


def _stream_rows(n_rows, n_cols):
    fits = [r for r in range(BF16_SUBLANES, n_rows + 1, BF16_SUBLANES)
            if n_rows % r == 0 and r * n_cols * 4 <= STREAM_CHUNK_BYTES]
    return fits[-1]


def _stream_to_bf16(src, dst):
    rows = _stream_rows(*src.shape)
    n_chunks = src.shape[0] // rows

    def scoped(stage, sem):
        def copy(i, slot):
            return pltpu.make_async_copy(src.at[pl.ds(i * rows, rows), :], stage.at[slot],
                                         sem.at[slot])

        for j in range(min(STREAM_DEPTH - 1, n_chunks)):
            copy(j, j).start()

        def body(i, carry):
            ahead = i + (STREAM_DEPTH - 1)

            @pl.when(ahead < n_chunks)
            def _():
                copy(ahead, lax.rem(ahead, STREAM_DEPTH)).start()

            slot = lax.rem(i, STREAM_DEPTH)
            copy(i, slot).wait()
            dst[pl.ds(pl.multiple_of(i * rows, rows), rows), :] = stage[slot].astype(BF16)
            return carry

        lax.fori_loop(0, n_chunks, body, 0)

    pl.run_scoped(scoped, pltpu.VMEM((STREAM_DEPTH, rows, src.shape[1]), F32),
                  pltpu.SemaphoreType.DMA((STREAM_DEPTH,)))


def _first_step():
    return (pl.program_id(0) == 0) & (pl.program_id(1) == 0)


def _layer_rows(layer, *refs):
    return [r.at[pl.ds(layer, 1), :] for r in refs]


def _mixer_kernel(x_ref, g_all, lng_all, lnb_all, cb_all, clg_all, clb_all, ps_all,
                  win_hbm, ws_ref, bs_ref, cw_ref, sw_ref, pw_ref,
                  wbr_hbm, wout_hbm, o_ref,
                  win_ref, wbr_ref, wout_ref,
                  h_buf, y_buf, gate_buf, conf_buf, short_buf, pool_buf, *, layer):
    s = pl.program_id(1)
    tm = x_ref.shape[0]
    g_ref, lng_ref, lnb_ref, cb_ref, clg_ref, clb_ref, ps_ref = _layer_rows(
        layer, g_all, lng_all, lnb_all, cb_all, clg_all, clb_all, ps_all)

    @pl.when(_first_step())
    def _():
        _stream_to_bf16(win_hbm.at[layer], win_ref)
        _stream_to_bf16(wbr_hbm.at[layer], wbr_ref)
        _stream_to_bf16(wout_hbm.at[layer], wout_ref)

    @pl.when(s == 0)
    def _():
        conf_buf[:, 0:CONF_HALO, :] = jnp.zeros((MIX_TILES, CONF_HALO, LANES), F32)
        short_buf[:, 0:SHORT_HALO, :] = jnp.zeros((MIX_TILES, SHORT_HALO, LANES), F32)
        pool_buf[:, 0:POOL_HALO, :] = jnp.zeros((MIX_TILES, POOL_HALO, LANES), F32)

    def store_tiles(buf, halo, value):
        for c in range(MIX_TILES):
            buf[c, halo:halo + tm, :] = value[:, c * LANES:(c + 1) * LANES]

    h_buf[...] = _rms_norm(x_ref[...], g_ref[...]).astype(BF16)

    z = _dot(h_buf[...], win_ref[:, 0:G_OFF])
    zb = z[:, B_OFF:B_OFF + 2 * BW]
    store_tiles(conf_buf, CONF_HALO, zb[:, :BW] * jax.nn.sigmoid(zb[:, BW:]))
    zc = z[:, C_OFF:C_OFF + 3 * BW]
    store_tiles(short_buf, SHORT_HALO, zc[:, BW:2 * BW] * zc[:, 2 * BW:])
    zd = z[:, D_OFF:D_OFF + BW]
    store_tiles(pool_buf, POOL_HALO, zd)
    za = jax.nn.gelu(z[:, A_OFF:A_OFF + 2 * BW])

    def merge_gate(k):
        zg = _dot(h_buf[...], win_ref[:, G_OFF + k * D_MODEL:G_OFF + (k + 1) * D_MODEL])
        gate_buf[k] = jax.nn.sigmoid(zg)

    merge_gate(0)
    merge_gate(1)

    def causal_taps(buf, halo, w_ref):
        n_taps = w_ref.shape[0]
        base = halo - (n_taps - 1)
        outs = []
        for c in range(MIX_TILES):
            lanes = slice(c * LANES, (c + 1) * LANES)
            acc = buf[c, base:base + tm, :] * w_ref[0:1, lanes]
            for k in range(1, n_taps):
                acc = acc + buf[c, base + k:base + k + tm, :] * w_ref[k:k + 1, lanes]
            buf[c, 0:halo, :] = buf[c, tm:tm + halo, :]
            outs.append(acc)
        return jnp.concatenate(outs, axis=1)

    yb = causal_taps(conf_buf, CONF_HALO, cw_ref) + cb_ref[...]

    y_buf[2] = (zc[:, :BW] * causal_taps(short_buf, SHORT_HALO, sw_ref)).astype(BF16)

    t_pos = s * tm + lax.broadcasted_iota(jnp.int32, (tm, LANES), 0) + 1
    low = lax.broadcasted_iota(jnp.int32, (tm, LANES), 1) < GROUP_DIM
    span = POOL_HALO - SUBLANES + tm
    pooled = []
    for c in range(MIX_TILES):
        w_lo, w_hi = POOL_WINDOWS[2 * c], POOL_WINDOWS[2 * c + 1]
        carry_rows = pool_buf[c, tm:tm + POOL_HALO, :]
        w = 1
        while w < w_hi:
            doubled = (pool_buf[c, SUBLANES:SUBLANES + span, :]
                       + pool_buf[c, SUBLANES - w:SUBLANES - w + span, :])
            pool_buf[c, SUBLANES:SUBLANES + span, :] = doubled
            w *= 2
            if w == w_lo:
                lo_sum = doubled[POOL_HALO - SUBLANES:, :]
        win_sum = jnp.where(low, lo_sum, doubled[POOL_HALO - SUBLANES:, :])
        cnt = jnp.minimum(t_pos, jnp.where(low, w_lo, w_hi)).astype(F32)
        pooled.append(win_sum / cnt - zd[:, c * LANES:(c + 1) * LANES])
        pool_buf[c, 0:POOL_HALO, :] = carry_rows
    pooled = jnp.concatenate(pooled, axis=1).astype(BF16)

    y_buf[3] = (_dot(pooled, pw_ref[...]) * ps_ref[...]).astype(BF16)
    merge_gate(2)
    gi = lax.broadcasted_iota(jnp.int32, (BW, BW), 0) >> GROUP_SHIFT
    gj = lax.broadcasted_iota(jnp.int32, (BW, BW), 1) >> GROUP_SHIFT
    g_avg = jnp.where(gi == gj, 1.0 / GROUP_DIM, 0.0).astype(BF16)
    ybc = yb - _dot(yb.astype(BF16), g_avg)
    merge_gate(3)

    u = za[:, :BW]
    v = za[:, BW:]
    mu = jnp.mean(v, axis=-1, keepdims=True)
    vc = v - mu
    var = jnp.mean(vc * vc, axis=-1, keepdims=True)
    v = (vc * lax.rsqrt(var + EPS) * lng_ref[...] + lnb_ref[...]).astype(BF16)
    row = lax.broadcasted_iota(jnp.int32, (CHUNK, N_GROUPS * CHUNK), 0)
    col = lax.broadcasted_iota(jnp.int32, (CHUNK, N_GROUPS * CHUNK), 1)
    w_all = jnp.concatenate([ws_ref[g] for g in range(N_GROUPS)], axis=1).astype(BF16)
    w_cat = jnp.where((col & (CHUNK - 1)) <= row, w_all, jnp.zeros((), BF16))
    lane_group = lax.broadcasted_iota(jnp.int32, (CHUNK, BW), 1) >> GROUP_SHIFT
    for n in range(tm // CHUNK):
        vn = v[n * CHUNK:(n + 1) * CHUNK, :]
        v_bd = jnp.concatenate(
            [jnp.where(lane_group == g, vn, jnp.zeros((), BF16)) for g in range(N_GROUPS)],
            axis=0)
        mixed = _dot(w_cat, v_bd) + bs_ref[...]
        y_buf[0, n * CHUNK:(n + 1) * CHUNK, :] = (
            u[n * CHUNK:(n + 1) * CHUNK, :] * mixed).astype(BF16)
    gvar = _dot((ybc * ybc).astype(BF16), g_avg)
    ybn = ybc * lax.rsqrt(gvar + EPS) * clg_ref[...] + clb_ref[...]
    y_buf[1] = (ybn * jax.nn.sigmoid(ybn)).astype(BF16)

    half = tm // 2
    for r in range(2):
        rows = slice(r * half, (r + 1) * half)
        merged = None
        for k in (2, 3, 0, 1):
            term = _dot(y_buf[k, rows, :], wbr_ref[k * BW:(k + 1) * BW, :]) * gate_buf[k, rows, :]
            merged = term if merged is None else merged + term
        o_ref[rows, :] = x_ref[rows, :] + _dot(merged.astype(BF16), wout_ref[...])


def _ffn_kernel(x_ref, p_ref, gffn_all, bconv_all, gple_all, gfin_ref,
                wup_hbm, wconv_ref, wdown_hbm, wpg_hbm, wpp_hbm, o_ref,
                wup_ref, wdown_ref, wpg_ref, wpp_ref,
                h_buf, act_buf, acc_buf, up_buf, *, layer, final_norm):
    s = pl.program_id(1)
    tm = x_ref.shape[0]
    gffn_ref, bconv_ref, gple_ref = _layer_rows(layer, gffn_all, bconv_all, gple_all)

    @pl.when(_first_step())
    def _():
        _stream_to_bf16(wup_hbm.at[layer], wup_ref)
        _stream_to_bf16(wdown_hbm.at[layer], wdown_ref)
        _stream_to_bf16(wpg_hbm.at[layer], wpg_ref)
        _stream_to_bf16(wpp_hbm.at[layer], wpp_ref)

    @pl.when(s == 0)
    def _():
        up_buf[:, 0:SHORT_HALO, :] = jnp.zeros((2 * D_FF // LANES, SHORT_HALO, LANES), F32)

    x = x_ref[...]
    h_buf[...] = _rms_norm(x, gffn_ref[...]).astype(BF16)

    base = SHORT_HALO - (FFN_K - 1)

    def up_chunk(j):
        parts = []
        for part in range(2):
            c0 = part * D_FF + j * FFN_COLS
            up = _dot(h_buf[...], wup_ref[:, c0:c0 + FFN_COLS])
            convs = []
            for c in range(FFN_COLS // LANES):
                tile = c0 // LANES + c
                lanes = slice(tile * LANES, (tile + 1) * LANES)
                up_buf[tile, SHORT_HALO:SHORT_HALO + tm, :] = up[:, c * LANES:(c + 1) * LANES]
                conv = up_buf[tile, base:base + tm, :] * wconv_ref[0:1, lanes]
                for k in range(1, FFN_K):
                    conv = conv + up_buf[tile, base + k:base + k + tm, :] * wconv_ref[k:k + 1, lanes]
                up_buf[tile, 0:SHORT_HALO, :] = up_buf[tile, tm:tm + SHORT_HALO, :]
                convs.append(conv)
            parts.append(jnp.concatenate(convs, axis=1) + bconv_ref[:, c0:c0 + FFN_COLS])
        gate, val = parts
        act_buf[:, j * FFN_COLS:(j + 1) * FFN_COLS] = (
            gate * jax.nn.sigmoid(gate) * val).astype(BF16)

    def down_group(first, last):
        rows = slice(first * FFN_COLS, last * FFN_COLS)
        partial = _dot(act_buf[:, rows], wdown_ref[rows, :])
        acc_buf[...] = (x if first == 0 else acc_buf[...]) + partial

    n_chunks = D_FF // FFN_COLS
    for j in range(n_chunks):
        up_chunk(j)
    for first in range(0, n_chunks, DOWN_GROUP):
        down_group(first, min(first + DOWN_GROUP, n_chunks))

    proj = _dot(p_ref[...].astype(BF16), wpp_ref[...])
    block = tm // PLE_ROW_BLOCKS
    for r in range(PLE_ROW_BLOCKS):
        rows = slice(r * block, (r + 1) * block)
        y = acc_buf[rows, :]
        h2 = _rms_norm(y, gple_ref[...]).astype(BF16)
        gate = jax.nn.sigmoid(_dot(h2, wpg_ref[...]))
        y = y + gate * proj[rows, :]
        if final_norm:
            y = _rms_norm(y, gfin_ref[...])
        o_ref[rows, :] = y


def _const_spec(shape, layer):
    nd = len(shape)
    return pl.BlockSpec((None,) + tuple(shape), lambda b, s: (layer,) + (0,) * nd,
                        pipeline_mode=pl.Buffered(1))


def _tile_spec(tm, width):
    return pl.BlockSpec((None, tm, width), lambda b, s: (b, s, 0))


def _compiler_params():
    return pltpu.CompilerParams(dimension_semantics=("arbitrary", "arbitrary"),
                                vmem_limit_bytes=VMEM_LIMIT_BYTES)


class _Streamed(NamedTuple):
    array: jax.Array


class _Whole(NamedTuple):
    array: jax.Array


def _param_specs(params, layer):
    specs, weight_bufs, operands = [], [], []
    for q in params:
        if isinstance(q, _Streamed):
            specs.append(pl.BlockSpec(memory_space=pl.ANY))
            weight_bufs.append(pltpu.VMEM(q.array.shape[1:], BF16))
            operands.append(q.array)
        elif isinstance(q, _Whole):
            specs.append(pl.BlockSpec(q.array.shape, lambda b, s, nd=q.array.ndim: (0,) * nd,
                                      pipeline_mode=pl.Buffered(1)))
            operands.append(q.array)
        else:
            specs.append(_const_spec(q.shape[1:], layer))
            operands.append(q)
    return specs, weight_bufs, operands


def _mixer_call(x, layer, params):
    batch, seq, _ = x.shape
    param_specs, weight_bufs, operands = _param_specs(params, layer)
    return pl.pallas_call(
        functools.partial(_mixer_kernel, layer=layer),
        grid=(batch, seq // TM_MIX),
        in_specs=[_tile_spec(TM_MIX, D_MODEL)] + param_specs,
        out_specs=_tile_spec(TM_MIX, D_MODEL),
        out_shape=jax.ShapeDtypeStruct(x.shape, F32),
        scratch_shapes=weight_bufs + [
            pltpu.VMEM((TM_MIX, D_MODEL), BF16),
            pltpu.VMEM((N_BRANCH, TM_MIX, BW), BF16),
            pltpu.VMEM((N_BRANCH, TM_MIX, D_MODEL), F32),
            pltpu.VMEM((MIX_TILES, CONF_HALO + TM_MIX, LANES), F32),
            pltpu.VMEM((MIX_TILES, SHORT_HALO + TM_MIX, LANES), F32),
            pltpu.VMEM((MIX_TILES, POOL_HALO + TM_MIX, LANES), F32),
        ],
        compiler_params=_compiler_params(),
        name=f"mixer_l{layer}",
    )(x, *operands)


def _ffn_call(x, p, layer, params, final_norm):
    batch, seq, _ = x.shape
    param_specs, weight_bufs, operands = _param_specs(params, layer)
    in_specs = ([_tile_spec(TM_FFN, D_MODEL),
                 pl.BlockSpec((None, None, TM_FFN, PLE_DIM), lambda b, s: (layer, b, s, 0))]
                + param_specs)
    return pl.pallas_call(
        functools.partial(_ffn_kernel, layer=layer, final_norm=final_norm),
        grid=(batch, seq // TM_FFN),
        in_specs=in_specs,
        out_specs=_tile_spec(TM_FFN, D_MODEL),
        out_shape=jax.ShapeDtypeStruct(x.shape, F32),
        scratch_shapes=weight_bufs + [
            pltpu.VMEM((TM_FFN, D_MODEL), BF16),
            pltpu.VMEM((TM_FFN, D_FF), BF16),
            pltpu.VMEM((TM_FFN, D_MODEL), F32),
            pltpu.VMEM((2 * D_FF // LANES, SHORT_HALO + TM_FFN, LANES), F32),
        ],
        compiler_params=_compiler_params(),
        name=f"ffn_l{layer}",
    )(x, p, *operands)


def kernel(x, p, g_mix, w_in, gmlp_ln_g, gmlp_ln_b, gmlp_w_s, gmlp_b_s, conf_w_dw, conf_b_dw, conf_ln_g, conf_ln_b, short_w, pool_w, pool_scale, w_branch, w_out, g_ffn, ffn_w_up, ffn_w_conv, ffn_b_conv, ffn_w_down, g_ple, ple_w_gate, ple_w_proj, g_final):
    depth = w_in.shape[0]
    assert x.shape[1] % TM_MIX == 0 and TM_MIX % CHUNK == 0 and x.shape[1] % TM_FFN == 0
    flat = lambda a: a.reshape(depth, -1)

    pool_bd = jnp.einsum('gh,lgcd->lgchd', jnp.eye(N_GROUPS, dtype=F32), pool_w)
    mixer_params = [
        _Whole(g_mix), _Whole(gmlp_ln_g), _Whole(gmlp_ln_b), _Whole(conf_b_dw),
        _Whole(flat(conf_ln_g)), _Whole(flat(conf_ln_b)), _Whole(flat(pool_scale)),
        _Streamed(w_in),
        gmlp_w_s,
        jnp.repeat(gmlp_b_s.transpose(0, 2, 1), GROUP_DIM, axis=2),
        conf_w_dw,
        short_w,
        pool_bd.reshape(depth, BW, BW).astype(BF16),
        _Streamed(w_branch.reshape(depth, N_BRANCH * BW, D_MODEL)), _Streamed(w_out),
    ]
    ffn_params = [
        _Whole(g_ffn), _Whole(ffn_b_conv), _Whole(g_ple), _Whole(g_final.reshape(1, D_MODEL)),
        _Streamed(ffn_w_up), ffn_w_conv,
        _Streamed(ffn_w_down), _Streamed(ple_w_gate), _Streamed(ple_w_proj),
    ]

    for i in range(depth):
        x = _mixer_call(x, i, mixer_params)
        x = _ffn_call(x, p, i, ffn_params, final_norm=(i == depth - 1))
    return x
```
